```python
import jax, jax.numpy as jnp
from jax import lax
import numpy as np

D_MODEL = 2048
BATCH = 8
SEQ = 2048
DEPTH = 4
DEC_BATCH = 32
DEC_SEQ = 16
PAST_LEN = 2048

CHUNK = 64
N_BRANCH = 4
BRANCH_W = D_MODEL // 4
RET_HEADS = 4
RET_DV = BRANCH_W // RET_HEADS
RET_DK = RET_DV // 2
ROPE_BASE = 10000.0
CONV_W = 3
GMLP_BLOCK = 128
GMLP_GROUPS = 4
GMLP_GC = BRANCH_W // GMLP_GROUPS
ATT_HEADS = 8
ATT_DH = BRANCH_W // ATT_HEADS
ATT_LEFT_CHUNKS = 8
ATT_BAND = ATT_LEFT_CHUNKS * CHUNK
MAX_REL = 128
NORM_EPS = 1e-6
IN_COLS = 2 * RET_HEADS * RET_DK + 13 * BRANCH_W + N_BRANCH * D_MODEL

kernel_name = "hybrid_streaming_encoder_step"


def split_in(h):
    bw = BRANCH_W
    sizes = (RET_HEADS * RET_DK, RET_HEADS * RET_DK, bw, bw,
             bw, bw, bw, bw,
             bw, bw, bw,
             bw, bw, bw, bw,
             N_BRANCH * D_MODEL)
    idx = np.cumsum(sizes)[:-1].tolist()
    return jnp.split(h, idx, axis=-1)


def rms_norm(x, g):
    xf = x.astype(jnp.float32)
    y = xf * lax.rsqrt(jnp.mean(xf * xf, axis=-1, keepdims=True) + NORM_EPS)
    return (y * g.astype(jnp.float32)).astype(x.dtype)


def layer_norm(x, g, b):
    xf = x.astype(jnp.float32)
    mu = jnp.mean(xf, axis=-1, keepdims=True)
    xc = xf - mu
    var = jnp.mean(xc * xc, axis=-1, keepdims=True)
    return (xc * lax.rsqrt(var + NORM_EPS) * g.astype(jnp.float32) + b.astype(jnp.float32)).astype(x.dtype)


def head_norm(o):
    of = o.astype(jnp.float32)
    mu = jnp.mean(of, axis=-1, keepdims=True)
    oc = of - mu
    var = jnp.mean(oc * oc, axis=-1, keepdims=True)
    return (oc * lax.rsqrt(var + NORM_EPS)).astype(o.dtype)


def rotary(x, pos):
    half = x.shape[-1] // 2
    freqs = ROPE_BASE ** (-jnp.arange(half, dtype=jnp.float32) / half)
    ang = pos.astype(jnp.float32)[:, None] * freqs[None, :]
    cos = jnp.cos(ang)[None, :, None, :]
    sin = jnp.sin(ang)[None, :, None, :]
    xf = x.astype(jnp.float32)
    x1, x2 = xf[..., :half], xf[..., half:]
    return jnp.concatenate([x1 * cos - x2 * sin, x1 * sin + x2 * cos], axis=-1).astype(x.dtype)


def retention_log_gamma():
    return jnp.log1p(-jnp.exp2(-5.0 - jnp.arange(RET_HEADS, dtype=jnp.float32)))


def retention_block(q, k, v, s_prev):
    L = q.shape[1]
    lg = retention_log_gamma()
    t = jnp.arange(L, dtype=jnp.float32)
    diff = t[:, None] - t[None, :]
    decay = jnp.where(diff >= 0, jnp.exp(jnp.maximum(diff, 0.0)[None] * lg[:, None, None]), 0.0).astype(q.dtype)
    q_decay = jnp.exp((t[None, :] + 1.0) * lg[:, None]).astype(q.dtype)
    k_decay = jnp.exp((L - 1.0 - t)[None, :] * lg[:, None]).astype(q.dtype)
    blk_decay = jnp.exp(L * lg).astype(q.dtype)
    inner = jnp.einsum('blhk,bmhk->bhlm', q, k) * decay[None]
    o = jnp.einsum('bhlm,bmhv->blhv', inner, v) + jnp.einsum('blhk,hl,bhkv->blhv', q, q_decay, s_prev)
    s_new = blk_decay[None, :, None, None] * s_prev + jnp.einsum('blhk,hl,blhv->bhkv', k, k_decay, v)
    return o, s_new


def retention_prompt(q, k, v):
    B, T, H, dk = q.shape
    dv = v.shape[-1]
    nc = T // CHUNK

    def to_chunks(a):
        return jnp.moveaxis(a.reshape(B, nc, CHUNK, H, a.shape[-1]), 1, 0)

    def step(s, qkv):
        o, s_new = retention_block(qkv[0], qkv[1], qkv[2], s)
        return s_new, o

    s0 = jnp.zeros((B, H, dk, dv), q.dtype)
    s_final, o = lax.scan(step, s0, (to_chunks(q), to_chunks(k), to_chunks(v)))
    return jnp.moveaxis(o, 0, 1).reshape(B, T, H, dv), s_final


def short_conv(z_ext, w):
    T = z_ext.shape[1] - (CONV_W - 1)
    y = w[0] * z_ext[:, 0:T]
    for i in range(1, CONV_W):
        y = y + w[i] * z_ext[:, i:i + T]
    return y


def gmlp_spatial(vn, ws, bs):
    L = vn.shape[2]
    pos = jnp.arange(L)
    mask = (pos[None, :] // CHUNK) <= (pos[:, None] // CHUNK)
    w = jnp.where(mask[None], ws[:, :L, :L], 0.0).astype(vn.dtype)
    return jnp.einsum('gij,bnjgc->bnigc', w, vn) + bs[:, :L].T[None, None, :, :, None]


def chunk_attention(q, k, v, rel, valid, table):
    s = jnp.einsum('bnqhd,bnkhd->bhnqk', q, k).astype(jnp.float32) * (ATT_DH ** -0.5)
    bias = table[:, jnp.clip(rel, -MAX_REL, MAX_REL) + MAX_REL].astype(jnp.float32)
    s = s + bias[None, :, None]
    s = jnp.where(valid[None, None], s, -1e30)
    p = jax.nn.softmax(s, axis=-1).astype(v.dtype)
    return jnp.einsum('bhnqk,bnkhd->bnqhd', p, v)


def merge_out(x, branches, gate_logits, w_branch, w_out):
    B, T, _ = x.shape
    gates = jax.nn.sigmoid(gate_logits.astype(jnp.float32)).astype(x.dtype).reshape(B, T, N_BRANCH, D_MODEL)
    h = gates[:, :, 0] * (branches[0] @ w_branch[0])
    for i in range(1, N_BRANCH):
        h = h + gates[:, :, i] * (branches[i] @ w_branch[i])
    return x + h @ w_out


def layer_prompt(x, gain, w_in, w_branch, w_out, conv_w, ln_g, ln_b, ws, bs, rel_table):
    B, T, _ = x.shape
    (rq, rk, rv, rg, cb, cc, cx, cg, mu, mv, mg, aq, ak, av, ag, gl) = split_in(rms_norm(x, gain) @ w_in)
    pos = jnp.arange(T)
    rq = rotary(rq.reshape(B, T, RET_HEADS, RET_DK), pos)
    rk = rotary(rk.reshape(B, T, RET_HEADS, RET_DK), pos) * (RET_DK ** -0.5)
    o_ret, s_ret = retention_prompt(rq, rk, rv.reshape(B, T, RET_HEADS, RET_DV))
    br_a = jax.nn.silu(rg) * head_norm(o_ret).reshape(B, T, BRANCH_W)
    z = cc * cx
    z_ext = jnp.pad(z, ((0, 0), (CONV_W - 1, 0), (0, 0)))
    br_b = jax.nn.silu(cg) * (cb * short_conv(z_ext, conv_w))
    conv_state = z_ext[:, -(CONV_W - 1):]
    vn = layer_norm(mv, ln_g, ln_b)
    mix = gmlp_spatial(vn.reshape(B, T // GMLP_BLOCK, GMLP_BLOCK, GMLP_GROUPS, GMLP_GC), ws, bs).reshape(B, T, BRANCH_W)
    br_c = jax.nn.silu(mg) * (mu * mix)
    nc = T // CHUNK
    q = aq.reshape(B, nc, CHUNK, ATT_HEADS, ATT_DH)
    k = ak.reshape(B, T, ATT_HEADS, ATT_DH)
    v = av.reshape(B, T, ATT_HEADS, ATT_DH)
    pad = ((0, 0), (ATT_BAND, 0), (0, 0), (0, 0))
    kp, vp = jnp.pad(k, pad), jnp.pad(v, pad)
    band = ATT_BAND + CHUNK
    idx = (jnp.arange(nc) * CHUNK)[:, None] + jnp.arange(band)[None, :]
    kb, vb = kp[:, idx], vp[:, idx]
    rel = jnp.arange(CHUNK)[:, None] - jnp.arange(band)[None, :] + ATT_BAND
    valid = (idx >= ATT_BAND)[:, None, :]
    o_att = chunk_attention(q, kb, vb, rel, valid, rel_table).reshape(B, T, BRANCH_W)
    br_d = jax.nn.silu(ag) * o_att
    keep = min(ATT_BAND, T)
    x = merge_out(x, (br_a, br_b, br_c, br_d), gl, w_branch, w_out)
    return x, s_ret, conv_state, k[:, T - keep:], v[:, T - keep:]


def layer_sample(x, ret_s, conv_s, cache_k, cache_v, gain, w_in, w_branch, w_out, conv_w, ln_g, ln_b, ws, bs, rel_table):
    B, L, _ = x.shape
    (rq, rk, rv, rg, cb, cc, cx, cg, mu, mv, mg, aq, ak, av, ag, gl) = split_in(rms_norm(x, gain) @ w_in)
    pos = PAST_LEN + jnp.arange(L)
    rq = rotary(rq.reshape(B, L, RET_HEADS, RET_DK), pos)
    rk = rotary(rk.reshape(B, L, RET_HEADS, RET_DK), pos) * (RET_DK ** -0.5)
    o_ret, s_ret = retention_block(rq, rk, rv.reshape(B, L, RET_HEADS, RET_DV), ret_s)
    br_a = jax.nn.silu(rg) * head_norm(o_ret).reshape(B, L, BRANCH_W)
    z = cc * cx
    z_ext = jnp.concatenate([conv_s.astype(z.dtype), z], axis=1)
    br_b = jax.nn.silu(cg) * (cb * short_conv(z_ext, conv_w))
    conv_state = z_ext[:, -(CONV_W - 1):]
    vn = layer_norm(mv, ln_g, ln_b)
    mix = gmlp_spatial(vn.reshape(B, 1, L, GMLP_GROUPS, GMLP_GC), ws, bs).reshape(B, L, BRANCH_W)
    br_c = jax.nn.silu(mg) * (mu * mix)
    W = cache_k.shape[1]
    q = aq.reshape(B, 1, L, ATT_HEADS, ATT_DH)
    k = ak.reshape(B, L, ATT_HEADS, ATT_DH)
    v = av.reshape(B, L, ATT_HEADS, ATT_DH)
    kk = jnp.concatenate([cache_k.astype(k.dtype), k], axis=1)[:, None]
    vv = jnp.concatenate([cache_v.astype(v.dtype), v], axis=1)[:, None]
    rel = jnp.arange(L)[:, None] - jnp.arange(W + L)[None, :] + W
    valid = jnp.ones((1, 1, W + L), dtype=bool)
    o_att = chunk_attention(q, kk, vv, rel, valid, rel_table).reshape(B, L, BRANCH_W)
    br_d = jax.nn.silu(ag) * o_att
    x = merge_out(x, (br_a, br_b, br_c, br_d), gl, w_branch, w_out)
    return x, s_ret, conv_state, vn, k, v


def setup_inputs(seed: int = 0) -> dict:
    key = jax.random.key(seed)
    ks = jax.random.split(key, 17)
    f32 = jnp.float32
    win_rows = min(ATT_BAND, PAST_LEN)
    n = lambda k, s: jax.random.normal(k, s, f32)
    return {
        "x_prompt": n(ks[0], (BATCH, SEQ, D_MODEL)),
        "x_sample": n(ks[1], (DEC_BATCH, DEC_SEQ, D_MODEL)),
        "state_ret": n(ks[2], (DEPTH, DEC_BATCH, RET_HEADS, RET_DK, RET_DV)),
        "state_conv": n(ks[3], (DEPTH, DEC_BATCH, CONV_W - 1, BRANCH_W)),
        "cache_att_k": n(ks[4], (DEPTH, DEC_BATCH, win_rows, ATT_HEADS, ATT_DH)),
        "cache_att_v": n(ks[5], (DEPTH, DEC_BATCH, win_rows, ATT_HEADS, ATT_DH)),
        "norm_gain": 1.0 + 0.01 * n(ks[6], (DEPTH, D_MODEL)),
        "w_in": n(ks[7], (DEPTH, D_MODEL, IN_COLS)) * (D_MODEL ** -0.5),
        "w_branch": n(ks[8], (DEPTH, N_BRANCH, BRANCH_W, D_MODEL)) * (BRANCH_W ** -0.5),
        "w_out": n(ks[9], (DEPTH, D_MODEL, D_MODEL)) * (D_MODEL ** -0.5),
        "conv_w": n(ks[10], (DEPTH, CONV_W, BRANCH_W)) * (CONV_W ** -0.5),
        "gmlp_ln_gain": 1.0 + 0.01 * n(ks[11], (DEPTH, BRANCH_W)),
        "gmlp_ln_bias": 0.01 * n(ks[12], (DEPTH, BRANCH_W)),
        "gmlp_ws": 0.5 * n(ks[13], (DEPTH, GMLP_GROUPS, GMLP_BLOCK, GMLP_BLOCK)) * (GMLP_BLOCK ** -0.5),
        "gmlp_bs": 1.0 + 0.1 * n(ks[14], (DEPTH, GMLP_GROUPS, GMLP_BLOCK)),
        "att_rel_bias": 0.5 * n(ks[15], (DEPTH, ATT_HEADS, 2 * MAX_REL + 1)),
        "final_norm_gain": 1.0 + 0.01 * n(ks[16], (D_MODEL,)),
    }


def reference(x_prompt, x_sample, state_ret, state_conv, cache_att_k, cache_att_v, norm_gain, w_in, w_branch,
              w_out, conv_w, gmlp_ln_gain, gmlp_ln_bias, gmlp_ws, gmlp_bs, att_rel_bias, final_norm_gain):
    p_ret, p_conv, p_k, p_v = [], [], [], []
    s_ret, s_conv, s_k, s_v, s_gv = [], [], [], [], []
    hp = x_prompt
    hs = x_sample
    for l in range(DEPTH):
        lw = (norm_gain[l], w_in[l], w_branch[l], w_out[l], conv_w[l], gmlp_ln_gain[l], gmlp_ln_bias[l],
              gmlp_ws[l], gmlp_bs[l], att_rel_bias[l])
        hp, a, b, c, d = layer_prompt(hp, *lw)
        p_ret.append(a); p_conv.append(b); p_k.append(c); p_v.append(d)
        hs, a, b, c, d, e = layer_sample(hs, state_ret[l], state_conv[l], cache_att_k[l], cache_att_v[l], *lw)
        s_ret.append(a); s_conv.append(b); s_gv.append(c); s_k.append(d); s_v.append(e)
    y_prompt = rms_norm(hp, final_norm_gain)
    y_sample = rms_norm(hs, final_norm_gain)
    return (y_prompt, y_sample,
            jnp.stack(p_ret), jnp.stack(p_conv), jnp.stack(p_k), jnp.stack(p_v),
            jnp.stack(s_ret), jnp.stack(s_conv), jnp.stack(s_k), jnp.stack(s_v), jnp.stack(s_gv))
```

```python
import functools

import numpy as np
import jax
import jax.numpy as jnp
from jax import lax
from jax.experimental import pallas as pl
from jax.experimental.pallas import tpu as pltpu

F32 = jnp.float32
BF16 = jnp.bfloat16

D_MODEL = 2048
BRANCH_W = 512
N_BRANCH = 4
CHUNK = 64
RET_HEADS = 4
RET_DK = 64
RET_DV = 128
ROPE_BASE = 10000.0
CONV_W = 3
GMLP_BLOCK = 128
GMLP_GROUPS = 4
ATT_HEADS = 8
ATT_DH = 64
ATT_BAND = 512
MAX_REL = 128
NORM_EPS = 1e-6
PAST_LEN = 2048
NEG_BIG = -1e30

MIX_COLS = 2 * RET_HEADS * RET_DK + 13 * BRANCH_W
IN_COLS = MIX_COLS + N_BRANCH * D_MODEL
C_RQ, C_RK, C_RV, C_RG = 0, 256, 512, 1024
C_CB, C_CC, C_CX, C_CG = 1536, 2048, 2560, 3072
C_MU, C_MV, C_MG = 3584, 4096, 4608
C_AQ, C_AK, C_AV, C_AG = 5120, 5632, 6144, 6656

LANES = 128
VMEM_LIMIT_BYTES = 56 * 1024 * 1024
TABLE_LANES = 384


def _cparams(n_axes):
    return pltpu.CompilerParams(dimension_semantics=("arbitrary",) * n_axes,
                                vmem_limit_bytes=VMEM_LIMIT_BYTES)


def _dot(a, b):
    return jnp.dot(a, b, preferred_element_type=F32)


def _dot_nt(a, b):
    return lax.dot_general(a, b, (((1,), (1,)), ((), ())), preferred_element_type=F32)


def _dot_tn(a, b):
    return lax.dot_general(a, b, (((0,), (0,)), ((), ())), preferred_element_type=F32)


def _silu(x):
    return x * jax.nn.sigmoid(x)


def _inproj_body(x_ref, g_ref, w_ref, h_ref, xn_ref):
    @pl.when(pl.program_id(1) == 0)
    def _():
        x = x_ref[...]
        r = lax.rsqrt(jnp.mean(x * x, axis=-1, keepdims=True) + NORM_EPS)
        xn_ref[...] = (x * r * g_ref[0]).astype(BF16)

    h_ref[...] = _dot(xn_ref[...], w_ref[0]).astype(h_ref.dtype)


def _inproj(x2d, gain3, w_in_bf, layer, tm, tn):
    n = x2d.shape[0]
    return pl.pallas_call(
        _inproj_body,
        grid=(n // tm, MIX_COLS // tn),
        in_specs=[
            pl.BlockSpec((tm, D_MODEL), lambda i, j: (i, 0)),
            pl.BlockSpec((1, 1, D_MODEL), lambda i, j: (layer, 0, 0)),
            pl.BlockSpec((1, D_MODEL, tn), lambda i, j: (layer, 0, j)),
        ],
        out_specs=[
            pl.BlockSpec((tm, tn), lambda i, j: (i, j)),
            pl.BlockSpec((tm, D_MODEL), lambda i, j: (i, 0)),
        ],
        out_shape=[
            jax.ShapeDtypeStruct((n, MIX_COLS), BF16),
            jax.ShapeDtypeStruct((n, D_MODEL), BF16),
        ],
        compiler_params=_cparams(2),
        name="inproj",
    )(x2d, gain3, w_in_bf)


def _merge_body(xn_ref, br_ref, x_ref, g0_ref, g1_ref, g2_ref, g3_ref, wb_ref, wo_ref, out_ref, hm_sc,
                *, nb, tnb):
    s = pl.program_id(1)
    gate_refs = (g0_ref, g1_ref, g2_ref, g3_ref)

    @pl.when(s < nb)
    def _():
        xn = xn_ref[...]
        acc = None
        for i in range(N_BRANCH):
            logits = _dot(xn, gate_refs[i][0])
            proj = _dot(br_ref[:, i * BRANCH_W:(i + 1) * BRANCH_W], wb_ref[0, i])
            term = jax.nn.sigmoid(logits) * proj
            acc = term if acc is None else acc + term
        hm_sc[s] = acc.astype(BF16)

    @pl.when(s >= nb)
    def _():
        acc = x_ref[...]
        for c in range(nb):
            acc = acc + _dot(hm_sc[c], wo_ref[0, c * tnb:(c + 1) * tnb, :])
        out_ref[...] = acc


def _merge(xn, br, x2d, w_in_bf, w_branch_bf, w_out_bf, layer, tm, tnb):
    n = x2d.shape[0]
    nb = D_MODEL // tnb
    gate_blk0 = MIX_COLS // tnb

    def gate_spec(i):
        return pl.BlockSpec((1, D_MODEL, tnb),
                            lambda m, s: (layer, 0, gate_blk0 + i * nb + jnp.minimum(s, nb - 1)))

    return pl.pallas_call(
        functools.partial(_merge_body, nb=nb, tnb=tnb),
        grid=(n // tm, 2 * nb),
        in_specs=[
            pl.BlockSpec((tm, D_MODEL), lambda m, s: (m, 0)),
            pl.BlockSpec((tm, D_MODEL), lambda m, s: (m, 0)),
            pl.BlockSpec((tm, tnb), lambda m, s: (m, jnp.maximum(s - nb, 0))),
            gate_spec(0), gate_spec(1), gate_spec(2), gate_spec(3),
            pl.BlockSpec((1, N_BRANCH, BRANCH_W, tnb), lambda m, s: (layer, 0, 0, jnp.minimum(s, nb - 1))),
            pl.BlockSpec((1, D_MODEL, tnb), lambda m, s: (layer, 0, jnp.maximum(s - nb, 0))),
        ],
        out_specs=pl.BlockSpec((tm, tnb), lambda m, s: (m, jnp.maximum(s - nb, 0))),
        out_shape=jax.ShapeDtypeStruct((n, D_MODEL), F32),
        scratch_shapes=[pltpu.VMEM((nb, tm, tnb), BF16)],
        compiler_params=_cparams(2),
        name="merge",
    )(xn, br, x2d, w_in_bf, w_in_bf, w_in_bf, w_in_bf, w_branch_bf, w_out_bf)


def _bias_body(tab_ref, perm_ref, out_ref, *, tq, w, w_valid, wpad, banded):
    t = tab_ref[0]
    perm = perm_ref[...]
    t_hi = t.astype(BF16)
    r1 = t - t_hi.astype(F32)
    t_mid = r1.astype(BF16)
    t_lo = (r1 - t_mid.astype(F32)).astype(BF16)
    row0 = _dot(t_hi, perm) + _dot(t_mid, perm) + _dot(t_lo, perm)
    qi = lax.broadcasted_iota(jnp.int32, (tq, w), 0)
    kj = lax.broadcasted_iota(jnp.int32, (tq, w), 1)
    visible = kj < w_valid
    if banded:
        lo = (qi // CHUNK) * CHUNK
        visible = visible & (kj >= lo) & (kj < lo + ATT_BAND + CHUNK)
    for h in range(ATT_HEADS):
        full = jnp.broadcast_to(row0[h:h + 1, :], (tq, wpad))
        shifted = pltpu.roll(full, 0, 1, stride=1, stride_axis=0)
        out_ref[h] = jnp.where(visible, shifted[:, :w], NEG_BIG)


def _bias_perm(w, wpad):
    j = np.arange(wpad)
    m = np.clip(ATT_BAND - j, -MAX_REL, MAX_REL) + MAX_REL
    m = np.where(j >= w, 2 * MAX_REL, m)
    perm = np.zeros((TABLE_LANES, wpad), np.float32)
    perm[m, j] = 1.0
    return jnp.asarray(perm, BF16)


def _rel_bias(table_pad, layer, tq, w, w_valid, banded):
    wpad = -(-(w + tq) // LANES) * LANES
    return pl.pallas_call(
        functools.partial(_bias_body, tq=tq, w=w, w_valid=w_valid, wpad=wpad, banded=banded),
        grid=(1,),
        in_specs=[
            pl.BlockSpec((1, ATT_HEADS, TABLE_LANES), lambda i: (layer, 0, 0)),
            pl.BlockSpec((TABLE_LANES, wpad), lambda i: (0, 0)),
        ],
        out_specs=pl.BlockSpec((ATT_HEADS, tq, w), lambda i: (0, 0, 0)),
        out_shape=jax.ShapeDtypeStruct((ATT_HEADS, tq, w), F32),
        compiler_params=_cparams(1),
        name="rel_bias",
    )(table_pad, _bias_perm(w, wpad))


def _rotary(x, cos, sin_signed):
    lane = lax.broadcasted_iota(jnp.int32, x.shape, 1)
    first_half = (lane % RET_DK) < (RET_DK // 2)
    width = x.shape[1]
    swapped = jnp.where(first_half, pltpu.roll(x, width - RET_DK // 2, 1), pltpu.roll(x, RET_DK // 2, 1))
    return x * cos + swapped * sin_signed


def _retention(h_ref, cos, sin_signed, dmat_ref, qdec, kdec, blk_ref, smask, states, br_ref):
    length = h_ref.shape[0]
    q = _rotary(h_ref[:, C_RQ:C_RQ + 256].astype(F32), cos, sin_signed)
    k = _rotary(h_ref[:, C_RK:C_RK + 256].astype(F32), cos, sin_signed) * (RET_DK ** -0.5)
    kb = k.astype(BF16)
    qd, kd = (q * qdec).astype(BF16), (k * kdec).astype(BF16)
    lane = lax.broadcasted_iota(jnp.int32, (length, LANES), 1)
    new_states = []
    for p in range(RET_HEADS // 2):
        cols = slice(p * LANES, (p + 1) * LANES)
        sp = states[p]
        o_inter = _dot(qd[:, cols], sp.astype(BF16))
        for half in range(2):
            hh = 2 * p + half
            in_head = (lane >= half * RET_DK) & (lane < (half + 1) * RET_DK)
            qm = jnp.where(in_head, q[:, cols], 0.0).astype(BF16)
            inner = (_dot_nt(qm, kb[:, cols]) * dmat_ref[hh]).astype(BF16)
            vcols = slice(C_RV + hh * RET_DV, C_RV + (hh + 1) * RET_DV)
            o = _dot(inner, h_ref[:, vcols]) + o_inter[:, half * RET_DV:(half + 1) * RET_DV]
            mu = jnp.mean(o, axis=-1, keepdims=True)
            oc = o - mu
            var = jnp.mean(oc * oc, axis=-1, keepdims=True)
            hn = oc * lax.rsqrt(var + NORM_EPS)
            gcols = slice(C_RG + hh * RET_DV, C_RG + (hh + 1) * RET_DV)
            br_ref[:, hh * RET_DV:(hh + 1) * RET_DV] = (_silu(h_ref[:, gcols].astype(F32)) * hn).astype(BF16)
        v_pair = h_ref[:, C_RV + p * 2 * RET_DV:C_RV + (p + 1) * 2 * RET_DV]
        upd = _dot_tn(kd[:, cols], v_pair)
        new_states.append((sp * blk_ref[p] + upd) * smask)
    return new_states


def _conv(h_ref, carry0, carry1, convw, br_ref):
    length = h_ref.shape[0]
    z = h_ref[:, C_CC:C_CC + BRANCH_W].astype(F32) * h_ref[:, C_CX:C_CX + BRANCH_W].astype(F32)
    row = lax.broadcasted_iota(jnp.int32, z.shape, 0)
    z1 = jnp.where(row == 0, carry1, pltpu.roll(z, 1, 0))
    z2 = jnp.where(row == 0, carry0, jnp.where(row == 1, carry1, pltpu.roll(z, 2, 0)))
    y = convw[0:1] * z2 + convw[1:2] * z1 + convw[2:3] * z
    cb = h_ref[:, C_CB:C_CB + BRANCH_W].astype(F32)
    cg = h_ref[:, C_CG:C_CG + BRANCH_W].astype(F32)
    br_ref[:, BRANCH_W:2 * BRANCH_W] = (_silu(cg) * (cb * y)).astype(BF16)
    del length
    return z


def _gmlp(h_ref, lng, lnb, ws_ref, bst_ref, br_ref, blk_len):
    length = h_ref.shape[0]
    mv = h_ref[:, C_MV:C_MV + BRANCH_W].astype(F32)
    mean = jnp.mean(mv, axis=-1, keepdims=True)
    cen = mv - mean
    var = jnp.mean(cen * cen, axis=-1, keepdims=True)
    vn = cen * lax.rsqrt(var + NORM_EPS) * lng + lnb
    vnb = vn.astype(BF16)
    ri = lax.broadcasted_iota(jnp.int32, (blk_len, blk_len), 0)
    ci = lax.broadcasted_iota(jnp.int32, (blk_len, blk_len), 1)
    causal = (ci // CHUNK) <= (ri // CHUNK)
    for g in range(GMLP_GROUPS):
        wg = jnp.where(causal, ws_ref[0, g, 0:blk_len, 0:blk_len], 0.0).astype(BF16)
        bcol = bst_ref[0, 0:blk_len, g:g + 1]
        for n in range(length // blk_len):
            rows = slice(n * blk_len, (n + 1) * blk_len)
            gc = slice(g * LANES, (g + 1) * LANES)
            mix = _dot(wg, vnb[rows, gc]) + bcol
            mu = h_ref[rows, C_MU + g * LANES:C_MU + (g + 1) * LANES].astype(F32)
            mg = h_ref[rows, C_MG + g * LANES:C_MG + (g + 1) * LANES].astype(F32)
            br_ref[rows, 2 * BRANCH_W + g * LANES:2 * BRANCH_W + (g + 1) * LANES] = (
                _silu(mg) * (mu * mix)).astype(BF16)
    return vn


def _attention(h_ref, kh_ref, vh_ref, bias_ref, key_valid, br_ref):
    length = h_ref.shape[0]
    lane = lax.broadcasted_iota(jnp.int32, (length, LANES), 1)
    for p in range(ATT_HEADS // 2):
        cols = slice(p * LANES, (p + 1) * LANES)
        qp = h_ref[:, C_AQ + p * LANES:C_AQ + (p + 1) * LANES].astype(F32)
        kp = kh_ref[:, cols]
        vp = vh_ref[:, cols]
        acc = None
        for half in range(2):
            hh = 2 * p + half
            in_head = (lane >= half * ATT_DH) & (lane < (half + 1) * ATT_DH)
            qm = jnp.where(in_head, qp, 0.0).astype(BF16)
            s = _dot_nt(qm, kp) * (ATT_DH ** -0.5) + bias_ref[hh]
            if key_valid is not None:
                s = jnp.where(key_valid, s, NEG_BIG)
            m = jnp.max(s, axis=-1, keepdims=True)
            e = jnp.exp(s - m)
            denom = jnp.sum(e, axis=-1, keepdims=True)
            o = _dot(e.astype(BF16), vp) / denom
            acc = o if acc is None else jnp.where(in_head, o, acc)
        ag = h_ref[:, C_AG + p * LANES:C_AG + (p + 1) * LANES].astype(F32)
        br_ref[:, 3 * BRANCH_W + p * LANES:3 * BRANCH_W + (p + 1) * LANES] = (_silu(ag) * acc).astype(BF16)


def _state_to_pairs(state_ref):
    pairs = []
    zero = jnp.zeros((RET_DK, RET_DV), F32)
    for p in range(RET_HEADS // 2):
        top = jnp.concatenate([state_ref[2 * p], zero], axis=1)
        bot = jnp.concatenate([zero, state_ref[2 * p + 1]], axis=1)
        pairs.append(jnp.concatenate([top, bot], axis=0))
    return pairs


def _pairs_to_state(pairs, out_ref):
    for p in range(RET_HEADS // 2):
        out_ref[2 * p] = pairs[p][0:RET_DK, 0:RET_DV]
        out_ref[2 * p + 1] = pairs[p][RET_DK:2 * RET_DK, RET_DV:2 * RET_DV]


def _mixer_prompt_body(h_ref, cos_ref, sin_ref, dmat_ref, qdec_ref, kdec_ref, blk_ref, smask_ref,
                       convw_ref, lng_ref, lnb_ref, ws_ref, bst_ref, bias_ref,
                       br_ref, sret_ref, sconv_ref, ko_ref, vo_ref,
                       sp_sc, zc_sc, kh_sc, vh_sc, *, tt):
    t = pl.program_id(1)

    @pl.when(t == 0)
    def _():
        sp_sc[...] = jnp.zeros_like(sp_sc)
        zc_sc[...] = jnp.zeros_like(zc_sc)
        kh_sc[0:ATT_BAND] = jnp.zeros((ATT_BAND, BRANCH_W), BF16)
        vh_sc[0:ATT_BAND] = jnp.zeros((ATT_BAND, BRANCH_W), BF16)

    states = _retention(h_ref, cos_ref[...], sin_ref[...], dmat_ref, qdec_ref[...], kdec_ref[...], blk_ref,
                        smask_ref[...], [sp_sc[0], sp_sc[1]], br_ref)
    sp_sc[0] = states[0]
    sp_sc[1] = states[1]
    _pairs_to_state(states, sret_ref.at[0])

    z = _conv(h_ref, zc_sc[0:1], zc_sc[1:2], convw_ref[0], br_ref)
    tail = z[tt - (CONV_W - 1):tt]
    zc_sc[0:CONV_W - 1] = tail
    sconv_ref[0] = tail

    _gmlp(h_ref, lng_ref[0], lnb_ref[0], ws_ref, bst_ref, br_ref, GMLP_BLOCK)

    ak = h_ref[:, C_AK:C_AK + BRANCH_W]
    av = h_ref[:, C_AV:C_AV + BRANCH_W]
    kh_sc[ATT_BAND:ATT_BAND + tt] = ak
    vh_sc[ATT_BAND:ATT_BAND + tt] = av
    ko_ref[0] = ak.astype(F32)
    vo_ref[0] = av.astype(F32)
    col = lax.broadcasted_iota(jnp.int32, (tt, ATT_BAND + tt), 1)
    key_valid = col >= ATT_BAND - t * tt
    _attention(h_ref, kh_sc, vh_sc, bias_ref, key_valid, br_ref)
    k_keep = kh_sc[tt:tt + ATT_BAND]
    v_keep = vh_sc[tt:tt + ATT_BAND]
    kh_sc[0:ATT_BAND] = k_keep
    vh_sc[0:ATT_BAND] = v_keep


def _mixer_prompt(h, tabs, conv_w, ln_g3, ln_b3, ws, bst, bias, layer, batch, seq, tt):
    nt = seq // tt
    t_keep = (seq - ATT_BAND) // tt
    full = lambda shape: pl.BlockSpec(shape, lambda b, t: (0,) * len(shape))
    per_layer = lambda shape: pl.BlockSpec((1,) + shape, lambda b, t: (layer,) + (0,) * len(shape))
    w = ATT_BAND + tt
    return pl.pallas_call(
        functools.partial(_mixer_prompt_body, tt=tt),
        grid=(batch, nt),
        in_specs=[
            pl.BlockSpec((tt, MIX_COLS), lambda b, t: (b * nt + t, 0)),
            pl.BlockSpec((tt, 256), lambda b, t: (t, 0)),
            pl.BlockSpec((tt, 256), lambda b, t: (t, 0)),
            full((RET_HEADS, tt, tt)),
            full((tt, 256)), full((tt, 256)),
            full((2, 1, 256)),
            full((2 * RET_DK, 2 * RET_DV)),
            per_layer((CONV_W, BRANCH_W)),
            per_layer((1, BRANCH_W)), per_layer((1, BRANCH_W)),
            per_layer((GMLP_GROUPS, GMLP_BLOCK, GMLP_BLOCK)),
            per_layer((GMLP_BLOCK, GMLP_GROUPS)),
            full((ATT_HEADS, tt, w)),
        ],
        out_specs=[
            pl.BlockSpec((tt, D_MODEL), lambda b, t: (b * nt + t, 0)),
            pl.BlockSpec((1, RET_HEADS, RET_DK, RET_DV), lambda b, t: (b, 0, 0, 0)),
            pl.BlockSpec((1, CONV_W - 1, BRANCH_W), lambda b, t: (b, 0, 0)),
            pl.BlockSpec((1, tt, BRANCH_W), lambda b, t: (b, jnp.maximum(t - t_keep, 0), 0)),
            pl.BlockSpec((1, tt, BRANCH_W), lambda b, t: (b, jnp.maximum(t - t_keep, 0), 0)),
        ],
        out_shape=[
            jax.ShapeDtypeStruct((batch * seq, D_MODEL), BF16),
            jax.ShapeDtypeStruct((batch, RET_HEADS, RET_DK, RET_DV), F32),
            jax.ShapeDtypeStruct((batch, CONV_W - 1, BRANCH_W), F32),
            jax.ShapeDtypeStruct((batch, ATT_BAND, BRANCH_W), F32),
            jax.ShapeDtypeStruct((batch, ATT_BAND, BRANCH_W), F32),
        ],
        scratch_shapes=[
            pltpu.VMEM((2, 2 * RET_DK, 2 * RET_DV), F32),
            pltpu.VMEM((8, BRANCH_W), F32),
            pltpu.VMEM((w, BRANCH_W), BF16),
            pltpu.VMEM((w, BRANCH_W), BF16),
        ],
        compiler_params=_cparams(2),
        name="mixer_prompt",
    )(h, tabs["cos"], tabs["sin"], tabs["dmat"], tabs["qdec"], tabs["kdec"], tabs["blk"], tabs["smask"],
      conv_w, ln_g3, ln_b3, ws, bst, bias)


def _mixer_sample_body(h_ref, cos_ref, sin_ref, dmat_ref, qdec_ref, kdec_ref, blk_ref, smask_ref,
                       convw_ref, lng_ref, lnb_ref, ws_ref, bst_ref, bias_ref,
                       sret_in_ref, sconv_in_ref, ck_ref, cv_ref,
                       br_ref, sret_ref, sconv_ref, ko_ref, vo_ref, gv_ref,
                       kh_sc, vh_sc, *, length):
    states = _retention(h_ref, cos_ref[...], sin_ref[...], dmat_ref, qdec_ref[...], kdec_ref[...], blk_ref,
                        smask_ref[...], _state_to_pairs(sret_in_ref.at[0, 0]), br_ref)
    _pairs_to_state(states, sret_ref.at[0])

    z = _conv(h_ref, sconv_in_ref[0, 0, 0:1], sconv_in_ref[0, 0, 1:2], convw_ref[0], br_ref)
    sconv_ref[0] = z[length - (CONV_W - 1):length]

    gv_ref[0] = _gmlp(h_ref, lng_ref[0], lnb_ref[0], ws_ref, bst_ref, br_ref, length)

    ak = h_ref[:, C_AK:C_AK + BRANCH_W]
    av = h_ref[:, C_AV:C_AV + BRANCH_W]
    kh_sc[0:ATT_BAND] = ck_ref[0, 0].astype(BF16)
    vh_sc[0:ATT_BAND] = cv_ref[0, 0].astype(BF16)
    kh_sc[ATT_BAND:ATT_BAND + length] = ak
    vh_sc[ATT_BAND:ATT_BAND + length] = av
    pad_rows = kh_sc.shape[0] - (ATT_BAND + length)
    kh_sc[ATT_BAND + length:] = jnp.zeros((pad_rows, BRANCH_W), BF16)
    vh_sc[ATT_BAND + length:] = jnp.zeros((pad_rows, BRANCH_W), BF16)
    ko_ref[0] = ak.astype(F32)
    vo_ref[0] = av.astype(F32)
    _attention(h_ref, kh_sc, vh_sc, bias_ref, None, br_ref)


def _mixer_sample(h, tabs, conv_w, ln_g3, ln_b3, ws, bst, bias, state_ret, state_conv, cache_k, cache_v,
                  layer, streams, length):
    full = lambda shape: pl.BlockSpec(shape, lambda s: (0,) * len(shape))
    per_layer = lambda shape: pl.BlockSpec((1,) + shape, lambda s: (layer,) + (0,) * len(shape))
    per_stream = lambda shape: pl.BlockSpec((1, 1) + shape, lambda s: (layer, s) + (0,) * len(shape))
    out_stream = lambda shape: pl.BlockSpec((1,) + shape, lambda s: (s,) + (0,) * len(shape))
    w = bias.shape[-1]
    return pl.pallas_call(
        functools.partial(_mixer_sample_body, length=length),
        grid=(streams,),
        in_specs=[
            pl.BlockSpec((length, MIX_COLS), lambda s: (s, 0)),
            full((length, 256)), full((length, 256)),
            full((RET_HEADS, length, length)),
            full((length, 256)), full((length, 256)),
            full((2, 1, 256)),
            full((2 * RET_DK, 2 * RET_DV)),
            per_layer((CONV_W, BRANCH_W)),
            per_layer((1, BRANCH_W)), per_layer((1, BRANCH_W)),
            per_layer((GMLP_GROUPS, GMLP_BLOCK, GMLP_BLOCK)),
            per_layer((GMLP_BLOCK, GMLP_GROUPS)),
            full((ATT_HEADS, length, w)),
            per_stream((RET_HEADS, RET_DK, RET_DV)),
            per_stream((CONV_W - 1, BRANCH_W)),
            per_stream((ATT_BAND, BRANCH_W)),
            per_stream((ATT_BAND, BRANCH_W)),
        ],
        out_specs=[
            pl.BlockSpec((length, D_MODEL), lambda s: (s, 0)),
            out_stream((RET_HEADS, RET_DK, RET_DV)),
            out_stream((CONV_W - 1, BRANCH_W)),
            out_stream((length, BRANCH_W)),
            out_stream((length, BRANCH_W)),
            out_stream((length, BRANCH_W)),
        ],
        out_shape=[
            jax.ShapeDtypeStruct((streams * length, D_MODEL), BF16),
            jax.ShapeDtypeStruct((streams, RET_HEADS, RET_DK, RET_DV), F32),
            jax.ShapeDtypeStruct((streams, CONV_W - 1, BRANCH_W), F32),
            jax.ShapeDtypeStruct((streams, length, BRANCH_W), F32),
            jax.ShapeDtypeStruct((streams, length, BRANCH_W), F32),
            jax.ShapeDtypeStruct((streams, length, BRANCH_W), F32),
        ],
        scratch_shapes=[
            pltpu.VMEM((w, BRANCH_W), BF16),
            pltpu.VMEM((w, BRANCH_W), BF16),
        ],
        compiler_params=_cparams(1),
        name="mixer_sample",
    )(h, tabs["cos"], tabs["sin"], tabs["dmat"], tabs["qdec"], tabs["kdec"], tabs["blk"], tabs["smask"],
      conv_w, ln_g3, ln_b3, ws, bst, bias, state_ret, state_conv, cache_k, cache_v)


def _final_norm_body(x_ref, g_ref, o_ref):
    x = x_ref[...]
    r = lax.rsqrt(jnp.mean(x * x, axis=-1, keepdims=True) + NORM_EPS)
    o_ref[...] = x * r * g_ref[...]


def _final_norm(x2d, gain2, tm):
    n = x2d.shape[0]
    return pl.pallas_call(
        _final_norm_body,
        grid=(n // tm,),
        in_specs=[pl.BlockSpec((tm, D_MODEL), lambda i: (i, 0)), pl.BlockSpec((1, D_MODEL), lambda i: (0, 0))],
        out_specs=pl.BlockSpec((tm, D_MODEL), lambda i: (i, 0)),
        out_shape=jax.ShapeDtypeStruct((n, D_MODEL), F32),
        compiler_params=_cparams(1),
        name="final_norm",
    )(x2d, gain2)


def _mixer_tables(positions, blk_len):
    half = RET_DK // 2
    freqs = ROPE_BASE ** (-jnp.arange(half, dtype=F32) / half)
    ang = jnp.asarray(positions, F32)[:, None] * freqs[None, :]
    cos, sin = jnp.cos(ang), jnp.sin(ang)
    cos_t = jnp.tile(cos, (1, 2 * RET_HEADS))
    sin_t = jnp.tile(jnp.concatenate([-sin, sin], axis=1), (1, RET_HEADS))
    lg = jnp.log1p(-jnp.exp2(-5.0 - jnp.arange(RET_HEADS, dtype=F32)))
    t = jnp.arange(blk_len, dtype=F32)
    diff = t[:, None] - t[None, :]
    dmat = jnp.where(diff >= 0, jnp.exp(jnp.maximum(diff, 0.0)[None] * lg[:, None, None]), 0.0)
    qdec = jnp.repeat(jnp.exp((t[:, None] + 1.0) * lg[None, :]), RET_DK, axis=1)
    kdec = jnp.repeat(jnp.exp((blk_len - 1.0 - t)[:, None] * lg[None, :]), RET_DK, axis=1)
    blk = jnp.repeat(jnp.exp(blk_len * lg), RET_DV).reshape(RET_HEADS // 2, 1, 2 * RET_DV)
    smask = jnp.kron(jnp.eye(2, dtype=F32), jnp.ones((RET_DK, RET_DV), F32))
    return dict(cos=cos_t, sin=sin_t, dmat=dmat, qdec=qdec, kdec=kdec, blk=blk, smask=smask)


def _pick_tile(n, target):
    t = min(n, target)
    while n % t:
        t //= 2
    return t


def kernel(x_prompt, x_sample, state_ret, state_conv, cache_att_k, cache_att_v, norm_gain, w_in, w_branch, w_out,
           conv_w, gmlp_ln_gain, gmlp_ln_bias, gmlp_ws, gmlp_bs, att_rel_bias, final_norm_gain):
    batch, seq, _ = x_prompt.shape
    streams, dec_len, _ = x_sample.shape
    depth = w_in.shape[0]
    assert seq % 256 == 0 and seq >= ATT_BAND and cache_att_k.shape[2] == ATT_BAND

    tt = 256
    n_p, n_s = batch * seq, streams * dec_len
    tm_p, tm_s = _pick_tile(n_p, 1024), _pick_tile(n_s, 512)

    w_in_bf = w_in.astype(BF16)
    w_branch_bf = w_branch.astype(BF16)
    w_out_bf = w_out.astype(BF16)
    gain3 = norm_gain.reshape(depth, 1, D_MODEL)
    ln_g3 = gmlp_ln_gain.reshape(depth, 1, BRANCH_W)
    ln_b3 = gmlp_ln_bias.reshape(depth, 1, BRANCH_W)
    bst = jnp.swapaxes(gmlp_bs, 1, 2)
    table_pad = jnp.pad(att_rel_bias, ((0, 0), (0, 0), (0, TABLE_LANES - att_rel_bias.shape[-1])))
    cache_k = cache_att_k.reshape(depth, streams, ATT_BAND, BRANCH_W)
    cache_v = cache_att_v.reshape(depth, streams, ATT_BAND, BRANCH_W)

    tabs_p = _mixer_tables(np.arange(seq), tt)
    tabs_s = _mixer_tables(PAST_LEN + np.arange(dec_len), dec_len)

    hp = x_prompt.reshape(n_p, D_MODEL)
    hs = x_sample.reshape(n_s, D_MODEL)
    outs = [[] for _ in range(9)]
    for l in range(depth):
        bias_p = _rel_bias(table_pad, l, tt, ATT_BAND + tt, ATT_BAND + tt, True)
        w_s = -(-(ATT_BAND + dec_len) // LANES) * LANES
        bias_s = _rel_bias(table_pad, l, dec_len, w_s, ATT_BAND + dec_len, False)

        h_mix, xn = _inproj(hp, gain3, w_in_bf, l, tm_p, 1024)
        br, p_ret, p_conv, p_k, p_v = _mixer_prompt(h_mix, tabs_p, conv_w, ln_g3, ln_b3, gmlp_ws, bst, bias_p,
                                                    l, batch, seq, tt)
        hp = _merge(xn, br, hp, w_in_bf, w_branch_bf, w_out_bf, l, tm_p, 256)

        h_mix, xn = _inproj(hs, gain3, w_in_bf, l, tm_s, 1024)
        br, s_ret, s_conv, s_k, s_v, s_gv = _mixer_sample(h_mix, tabs_s, conv_w, ln_g3, ln_b3, gmlp_ws, bst, bias_s,
                                                          state_ret, state_conv, cache_k, cache_v,
                                                          l, streams, dec_len)
        hs = _merge(xn, br, hs, w_in_bf, w_branch_bf, w_out_bf, l, tm_s, 256)

        for dst, val in zip(outs, (p_ret, p_conv, p_k, p_v, s_ret, s_conv, s_k, s_v, s_gv)):
            dst.append(val)

    gain2 = final_norm_gain.reshape(1, D_MODEL)
    y_prompt = _final_norm(hp, gain2, _pick_tile(n_p, 512)).reshape(batch, seq, D_MODEL)
    y_sample = _final_norm(hs, gain2, _pick_tile(n_s, 512)).reshape(streams, dec_len, D_MODEL)
    p_ret, p_conv, p_k, p_v, s_ret, s_conv, s_k, s_v, s_gv = [jnp.stack(o) for o in outs]
    kv_shape = lambda a: a.reshape(a.shape[:3] + (ATT_HEADS, ATT_DH))
    return (y_prompt, y_sample, p_ret, p_conv, kv_shape(p_k), kv_shape(p_v),
            s_ret, s_conv, kv_shape(s_k), kv_shape(s_v), s_gv)
```

```python
import functools

import numpy as np
import jax
import jax.numpy as jnp
from jax import lax
from jax.experimental import pallas as pl
from jax.experimental.pallas import tpu as pltpu

F32 = jnp.float32
BF16 = jnp.bfloat16

D_MODEL = 2048
BRANCH_W = 512
N_BRANCH = 4
CHUNK = 64
RET_HEADS = 4
RET_DK = 64
RET_DV = 128
ROPE_BASE = 10000.0
CONV_W = 3
GMLP_BLOCK = 128
GMLP_GROUPS = 4
ATT_HEADS = 8
ATT_DH = 64
ATT_BAND = 512
MAX_REL = 128
NORM_EPS = 1e-6
PAST_LEN = 2048
NEG_BIG = -1e30

MIX_COLS = 2 * RET_HEADS * RET_DK + 13 * BRANCH_W
IN_COLS = MIX_COLS + N_BRANCH * D_MODEL
C_RQ, C_RK, C_RV, C_RG = 0, 256, 512, 1024
C_CB, C_CC, C_CX, C_CG = 1536, 2048, 2560, 3072
C_MU, C_MV, C_MG = 3584, 4096, 4608
C_AQ, C_AK, C_AV, C_AG = 5120, 5632, 6144, 6656

LANES = 128
WBLK = 256
INPROJ_TN = 7 * WBLK
ATT_ROWS = 128
MERGE_ROW_CHUNK = 256
SAMPLE_GROUP = 8
LOG2E = 1.4426950408889634
VMEM_LIMIT_BYTES = 56 * 1024 * 1024
TABLE_LANES = 384


def _cparams(n_axes):
    return pltpu.CompilerParams(dimension_semantics=("arbitrary",) * n_axes,
                                vmem_limit_bytes=VMEM_LIMIT_BYTES)


def _dot(a, b):
    return jnp.dot(a, b, preferred_element_type=F32)


def _dot_nt(a, b):
    return lax.dot_general(a, b, (((1,), (1,)), ((), ())), preferred_element_type=F32)


def _dot_tn(a, b):
    return lax.dot_general(a, b, (((0,), (0,)), ((), ())), preferred_element_type=F32)


def _silu(x):
    return x * jax.nn.sigmoid(x)


def _inproj_body(x_ref, g_ref, w_ref, h_ref, xn_ref, *, nsub):
    @pl.when(pl.program_id(1) == 0)
    def _():
        x = x_ref[...]
        r = lax.rsqrt(jnp.mean(x * x, axis=-1, keepdims=True) + NORM_EPS)
        xn_ref[...] = (x * r * g_ref[0]).astype(BF16)

    xn = xn_ref[...]
    for c in range(nsub):
        h_ref[:, c * WBLK:(c + 1) * WBLK] = _dot(xn, w_ref[0, c]).astype(h_ref.dtype)


def _inproj(x2d, gain3, w_mix_t, layer, tm, tn):
    n = x2d.shape[0]
    nsub = tn // WBLK
    return pl.pallas_call(
        functools.partial(_inproj_body, nsub=nsub),
        grid=(n // tm, MIX_COLS // tn),
        in_specs=[
            pl.BlockSpec((tm, D_MODEL), lambda i, j: (i, 0)),
            pl.BlockSpec((1, 1, D_MODEL), lambda i, j: (layer, 0, 0)),
            pl.BlockSpec((1, nsub, D_MODEL, WBLK), lambda i, j: (layer, j, 0, 0)),
        ],
        out_specs=[
            pl.BlockSpec((tm, tn), lambda i, j: (i, j)),
            pl.BlockSpec((tm, D_MODEL), lambda i, j: (i, 0)),
        ],
        out_shape=[
            jax.ShapeDtypeStruct((n, MIX_COLS), BF16),
            jax.ShapeDtypeStruct((n, D_MODEL), BF16),
        ],
        compiler_params=_cparams(2),
        name="inproj",
    )(x2d, gain3, w_mix_t)


MERGE_ROWS = N_BRANCH * D_MODEL + N_BRANCH * BRANCH_W + D_MODEL


def _merge_body(xn_ref, br_ref, x_ref, w_ref, *rest, final_norm):
    fg_ref, out_ref = rest if final_norm else (None, rest[0])
    s = pl.program_id(1)
    nb = D_MODEL // WBLK
    tm = out_ref.shape[0]

    @pl.when(s == 0)
    def _():
        out_ref[...] = jnp.zeros_like(out_ref)

    rc = min(tm, MERGE_ROW_CHUNK)
    wo0 = N_BRANCH * D_MODEL + N_BRANCH * BRANCH_W
    for r in range(tm // rc):
        rows = slice(r * rc, (r + 1) * rc)
        xn = xn_ref[rows, :]
        acc = None
        for i in range(N_BRANCH):
            logits = _dot(xn, w_ref[0, 0, i * D_MODEL:(i + 1) * D_MODEL, :])
            wb0 = N_BRANCH * D_MODEL + i * BRANCH_W
            proj = _dot(br_ref[rows, i * BRANCH_W:(i + 1) * BRANCH_W], w_ref[0, 0, wb0:wb0 + BRANCH_W, :])
            term = jax.nn.sigmoid(logits) * proj
            acc = term if acc is None else acc + term
        hm = acc.astype(BF16)
        for j in range(nb):
            out_ref[rows, j * WBLK:(j + 1) * WBLK] += _dot(hm, w_ref[0, 0, wo0 + j * WBLK:wo0 + (j + 1) * WBLK, :])

    for c in range(nb):
        @pl.when(s == c)
        def _():
            out_ref[:, c * WBLK:(c + 1) * WBLK] += x_ref[...]

    if final_norm:
        @pl.when(s == nb - 1)
        def _():
            y = out_ref[...]
            r = lax.rsqrt(jnp.mean(y * y, axis=-1, keepdims=True) + NORM_EPS)
            out_ref[...] = y * r * fg_ref[...]


def _merge(xn, br, x2d, w_merge_t, final_gain2, layer, tm, final_norm):
    n = x2d.shape[0]
    nb = D_MODEL // WBLK
    in_specs = [
        pl.BlockSpec((tm, D_MODEL), lambda m, s: (m, 0)),
        pl.BlockSpec((tm, D_MODEL), lambda m, s: (m, 0)),
        pl.BlockSpec((tm, WBLK), lambda m, s: (m, s)),
        pl.BlockSpec((1, 1, MERGE_ROWS, WBLK), lambda m, s: (layer, s, 0, 0)),
    ]
    operands = [xn, br, x2d, w_merge_t]
    if final_norm:
        in_specs.append(pl.BlockSpec((1, D_MODEL), lambda m, s: (0, 0)))
        operands.append(final_gain2)
    return pl.pallas_call(
        functools.partial(_merge_body, final_norm=final_norm),
        grid=(n // tm, nb),
        in_specs=in_specs,
        out_specs=pl.BlockSpec((tm, D_MODEL), lambda m, s: (m, 0)),
        out_shape=jax.ShapeDtypeStruct((n, D_MODEL), F32),
        compiler_params=_cparams(2),
        name="merge",
    )(*operands)


def _pack_mix_body(w_ref, o_ref, *, nsub):
    for c in range(nsub):
        o_ref[0, c] = w_ref[0, :, c * WBLK:(c + 1) * WBLK].astype(BF16)


def _pack_merge_body(g0_ref, g1_ref, g2_ref, g3_ref, wb_ref, wo_ref, o_ref):
    for i, g_ref in enumerate((g0_ref, g1_ref, g2_ref, g3_ref)):
        o_ref[0, 0, i * D_MODEL:(i + 1) * D_MODEL, :] = g_ref[0].astype(BF16)
    r0 = N_BRANCH * D_MODEL
    o_ref[0, 0, r0:r0 + N_BRANCH * BRANCH_W, :] = wb_ref[0].astype(BF16)
    r0 += N_BRANCH * BRANCH_W
    for j in range(D_MODEL // WBLK):
        o_ref[0, 0, r0 + j * WBLK:r0 + (j + 1) * WBLK, :] = wo_ref[0, :, j * WBLK:(j + 1) * WBLK].astype(BF16)


def _pack_weights(w_in, w_branch, w_out):
    depth = w_in.shape[0]
    nb = D_MODEL // WBLK
    nsub = INPROJ_TN // WBLK
    w_mix_t = pl.pallas_call(
        functools.partial(_pack_mix_body, nsub=nsub),
        grid=(depth, MIX_COLS // INPROJ_TN),
        in_specs=[pl.BlockSpec((1, D_MODEL, INPROJ_TN), lambda l, j: (l, 0, j))],
        out_specs=pl.BlockSpec((1, nsub, D_MODEL, WBLK), lambda l, j: (l, j, 0, 0)),
        out_shape=jax.ShapeDtypeStruct((depth, MIX_COLS // WBLK, D_MODEL, WBLK), BF16),
        compiler_params=_cparams(2),
        name="pack_mix",
    )(w_in)

    gate_blk0 = MIX_COLS // WBLK
    gate_spec = lambda i: pl.BlockSpec((1, D_MODEL, WBLK), lambda l, c: (l, 0, gate_blk0 + i * nb + c))
    w_merge_t = pl.pallas_call(
        _pack_merge_body,
        grid=(depth, nb),
        in_specs=[
            gate_spec(0), gate_spec(1), gate_spec(2), gate_spec(3),
            pl.BlockSpec((1, N_BRANCH * BRANCH_W, WBLK), lambda l, c: (l, 0, c)),
            pl.BlockSpec((1, WBLK, D_MODEL), lambda l, c: (l, c, 0)),
        ],
        out_specs=pl.BlockSpec((1, 1, MERGE_ROWS, WBLK), lambda l, c: (l, c, 0, 0)),
        out_shape=jax.ShapeDtypeStruct((depth, nb, MERGE_ROWS, WBLK), BF16),
        compiler_params=_cparams(2),
        name="pack_merge",
    )(w_in, w_in, w_in, w_in, w_branch.reshape(depth, N_BRANCH * BRANCH_W, D_MODEL), w_out)
    return w_mix_t, w_merge_t


def _bias_body(tab_ref, perm_ref, out_ref, *, n_var, tq, w, w_valid, wpad, banded):
    t = tab_ref[0] * LOG2E
    perm = perm_ref[...]
    t_hi = t.astype(BF16)
    r1 = t - t_hi.astype(F32)
    t_mid = r1.astype(BF16)
    t_lo = (r1 - t_mid.astype(F32)).astype(BF16)
    row0 = _dot(t_hi, perm) + _dot(t_mid, perm) + _dot(t_lo, perm)
    qi = lax.broadcasted_iota(jnp.int32, (tq, w), 0)
    kj = lax.broadcasted_iota(jnp.int32, (tq, w), 1)
    visible = kj < w_valid
    if banded:
        lo = (qi // CHUNK) * CHUNK
        visible = visible & (kj >= lo) & (kj < lo + ATT_BAND + CHUNK)
    for h in range(ATT_HEADS):
        full = jnp.broadcast_to(row0[h:h + 1, :], (tq, wpad))
        shifted = pltpu.roll(full, 0, 1, stride=1, stride_axis=0)[:, :w]
        for u in range(n_var):
            vis_u = visible & (kj >= ATT_BAND - u * tq) if banded else visible
            out_ref[0, u, h] = jnp.where(vis_u, shifted, NEG_BIG)


def _bias_perm(w, wpad):
    j = np.arange(wpad)
    m = np.clip(ATT_BAND - j, -MAX_REL, MAX_REL) + MAX_REL
    m = np.where(j >= w, 2 * MAX_REL, m)
    perm = np.zeros((TABLE_LANES, wpad), np.float32)
    perm[m, j] = 1.0
    return jnp.asarray(perm, BF16)


def _rel_bias(table_pad, n_var, tq, w, w_valid, banded):
    depth = table_pad.shape[0]
    wpad = -(-(w + tq) // LANES) * LANES
    return pl.pallas_call(
        functools.partial(_bias_body, n_var=n_var, tq=tq, w=w, w_valid=w_valid, wpad=wpad, banded=banded),
        grid=(depth,),
        in_specs=[
            pl.BlockSpec((1, ATT_HEADS, TABLE_LANES), lambda l: (l, 0, 0)),
            pl.BlockSpec((TABLE_LANES, wpad), lambda l: (0, 0)),
        ],
        out_specs=pl.BlockSpec((1, n_var, ATT_HEADS, tq, w), lambda l: (l, 0, 0, 0, 0)),
        out_shape=jax.ShapeDtypeStruct((depth, n_var, ATT_HEADS, tq, w), F32),
        compiler_params=_cparams(1),
        name="rel_bias",
    )(table_pad, _bias_perm(w, wpad))


def _rotary(x, cos, sin_signed):
    lane = lax.broadcasted_iota(jnp.int32, x.shape, 1)
    first_half = (lane % RET_DK) < (RET_DK // 2)
    width = x.shape[1]
    swapped = jnp.where(first_half, pltpu.roll(x, width - RET_DK // 2, 1), pltpu.roll(x, RET_DK // 2, 1))
    return x * cos + swapped * sin_signed


def _retention(h_ref, cos, sin_signed, dmat_ref, qdec, kdec, blk_ref, smask, states, br_ref):
    length = h_ref.shape[0]
    q = _rotary(h_ref[:, C_RQ:C_RQ + 256].astype(F32), cos, sin_signed)
    k = _rotary(h_ref[:, C_RK:C_RK + 256].astype(F32), cos, sin_signed) * (RET_DK ** -0.5)
    kb = k.astype(BF16)
    qd, kd = (q * qdec).astype(BF16), (k * kdec).astype(BF16)
    lane = lax.broadcasted_iota(jnp.int32, (length, LANES), 1)
    new_states = []
    for p in range(RET_HEADS // 2):
        cols = slice(p * LANES, (p + 1) * LANES)
        sp = states[p]
        o_inter = _dot(qd[:, cols], sp.astype(BF16))
        for half in range(2):
            hh = 2 * p + half
            in_head = (lane >= half * RET_DK) & (lane < (half + 1) * RET_DK)
            qm = jnp.where(in_head, q[:, cols], 0.0).astype(BF16)
            inner = (_dot_nt(qm, kb[:, cols]) * dmat_ref[hh]).astype(BF16)
            vcols = slice(C_RV + hh * RET_DV, C_RV + (hh + 1) * RET_DV)
            o = _dot(inner, h_ref[:, vcols]) + o_inter[:, half * RET_DV:(half + 1) * RET_DV]
            mu = jnp.mean(o, axis=-1, keepdims=True)
            oc = o - mu
            var = jnp.mean(oc * oc, axis=-1, keepdims=True)
            hn = oc * lax.rsqrt(var + NORM_EPS)
            gcols = slice(C_RG + hh * RET_DV, C_RG + (hh + 1) * RET_DV)
            br_ref[:, hh * RET_DV:(hh + 1) * RET_DV] = (_silu(h_ref[:, gcols].astype(F32)) * hn).astype(BF16)
        v_pair = h_ref[:, C_RV + p * 2 * RET_DV:C_RV + (p + 1) * 2 * RET_DV]
        upd = _dot_tn(kd[:, cols], v_pair)
        new_states.append((sp * blk_ref[p] + upd) * smask)
    return new_states


def _conv(h_ref, prev8, convw, br_ref):
    z = h_ref[:, C_CC:C_CC + BRANCH_W].astype(F32) * h_ref[:, C_CX:C_CX + BRANCH_W].astype(F32)
    zcat = jnp.concatenate([prev8, z], axis=0)
    z1 = pltpu.roll(zcat, 1, 0)[8:]
    z2 = pltpu.roll(zcat, 2, 0)[8:]
    y = convw[0:1] * z2 + convw[1:2] * z1 + convw[2:3] * z
    cb = h_ref[:, C_CB:C_CB + BRANCH_W].astype(F32)
    cg = h_ref[:, C_CG:C_CG + BRANCH_W].astype(F32)
    br_ref[:, BRANCH_W:2 * BRANCH_W] = (_silu(cg) * (cb * y)).astype(BF16)
    return z


def _gmlp(h_ref, lng, lnb, ws_ref, bst_ref, br_ref, blk_len):
    length = h_ref.shape[0]
    mv = h_ref[:, C_MV:C_MV + BRANCH_W].astype(F32)
    mean = jnp.mean(mv, axis=-1, keepdims=True)
    cen = mv - mean
    var = jnp.mean(cen * cen, axis=-1, keepdims=True)
    vn = cen * lax.rsqrt(var + NORM_EPS) * lng + lnb
    vnb = vn.astype(BF16)
    ri = lax.broadcasted_iota(jnp.int32, (blk_len, blk_len), 0)
    ci = lax.broadcasted_iota(jnp.int32, (blk_len, blk_len), 1)
    causal = (ci // CHUNK) <= (ri // CHUNK)
    for g in range(GMLP_GROUPS):
        wg = jnp.where(causal, ws_ref[0, g, 0:blk_len, 0:blk_len], 0.0).astype(BF16)
        bcol = bst_ref[0, 0:blk_len, g:g + 1]
        for n in range(length // blk_len):
            rows = slice(n * blk_len, (n + 1) * blk_len)
            gc = slice(g * LANES, (g + 1) * LANES)
            mix = _dot(wg, vnb[rows, gc]) + bcol
            mu = h_ref[rows, C_MU + g * LANES:C_MU + (g + 1) * LANES].astype(F32)
            mg = h_ref[rows, C_MG + g * LANES:C_MG + (g + 1) * LANES].astype(F32)
            br_ref[rows, 2 * BRANCH_W + g * LANES:2 * BRANCH_W + (g + 1) * LANES] = (
                _silu(mg) * (mu * mix)).astype(BF16)
    return vn


def _attention(h_ref, kh_ref, vh_ref, bias_ref, br_ref, r0, rows, c0, width):
    lane = lax.broadcasted_iota(jnp.int32, (rows, LANES), 1)
    rsl = slice(r0, r0 + rows)
    for p in range(ATT_HEADS // 2):
        cols = slice(p * LANES, (p + 1) * LANES)
        qp = h_ref[rsl, C_AQ + p * LANES:C_AQ + (p + 1) * LANES].astype(F32) * (LOG2E * ATT_DH ** -0.5)
        kp = kh_ref[c0:c0 + width, cols]
        vp = vh_ref[c0:c0 + width, cols]
        first = lane < ATT_DH
        q2 = jnp.concatenate([jnp.where(first, qp, 0.0), jnp.where(first, 0.0, qp)], axis=0).astype(BF16)
        s = _dot_nt(q2, kp) + bias_ref[0, 2 * p:2 * p + 2].reshape(2 * rows, width)
        m = jnp.max(s, axis=-1, keepdims=True)
        e = jnp.exp2(s - m)
        denom = jnp.sum(e, axis=-1, keepdims=True)
        o = _dot(e.astype(BF16), vp) / denom
        acc = jnp.where(first, o[0:rows], o[rows:2 * rows])
        ag = h_ref[rsl, C_AG + p * LANES:C_AG + (p + 1) * LANES].astype(F32)
        br_ref[rsl, 3 * BRANCH_W + p * LANES:3 * BRANCH_W + (p + 1) * LANES] = (_silu(ag) * acc).astype(BF16)


def _state_to_pairs(state_ref):
    pairs = []
    zero = jnp.zeros((RET_DK, RET_DV), F32)
    for p in range(RET_HEADS // 2):
        top = jnp.concatenate([state_ref[2 * p], zero], axis=1)
        bot = jnp.concatenate([zero, state_ref[2 * p + 1]], axis=1)
        pairs.append(jnp.concatenate([top, bot], axis=0))
    return pairs


def _pairs_to_state(pairs, out_ref):
    for p in range(RET_HEADS // 2):
        out_ref[2 * p] = pairs[p][0:RET_DK, 0:RET_DV]
        out_ref[2 * p + 1] = pairs[p][RET_DK:2 * RET_DK, RET_DV:2 * RET_DV]


def _mixer_prompt_body(h_ref, cos_ref, sin_ref, dmat_ref, qdec_ref, kdec_ref, blk_ref, smask_ref,
                       convw_ref, lng_ref, lnb_ref, ws_ref, bst_ref, bias0_ref, bias1_ref,
                       br_ref, sret_ref, sconv_ref, ko_ref, vo_ref,
                       sp_sc, zc_sc, kh_sc, vh_sc, *, tt):
    t = pl.program_id(1)

    @pl.when(t == 0)
    def _():
        sp_sc[...] = jnp.zeros_like(sp_sc)
        zc_sc[...] = jnp.zeros_like(zc_sc)
        kh_sc[0:ATT_BAND] = jnp.zeros((ATT_BAND, BRANCH_W), BF16)
        vh_sc[0:ATT_BAND] = jnp.zeros((ATT_BAND, BRANCH_W), BF16)

    states = _retention(h_ref, cos_ref[...], sin_ref[...], dmat_ref, qdec_ref[...], kdec_ref[...], blk_ref,
                        smask_ref[...], [sp_sc[0], sp_sc[1]], br_ref)
    sp_sc[0] = states[0]
    sp_sc[1] = states[1]
    _pairs_to_state(states, sret_ref.at[0])

    z = _conv(h_ref, zc_sc[...], convw_ref[0], br_ref)
    tail = z[tt - (CONV_W - 1):tt]
    zc_sc[8 - (CONV_W - 1):8] = tail
    sconv_ref[0] = tail

    _gmlp(h_ref, lng_ref[0], lnb_ref[0], ws_ref, bst_ref, br_ref, GMLP_BLOCK)

    ak = h_ref[:, C_AK:C_AK + BRANCH_W]
    av = h_ref[:, C_AV:C_AV + BRANCH_W]
    kh_sc[ATT_BAND:ATT_BAND + tt] = ak
    vh_sc[ATT_BAND:ATT_BAND + tt] = av
    ko_ref[0] = ak.astype(F32)
    vo_ref[0] = av.astype(F32)
    for blk, bias_ref in enumerate((bias0_ref, bias1_ref)):
        _attention(h_ref, kh_sc, vh_sc, bias_ref.at[0], br_ref, blk * ATT_ROWS, ATT_ROWS, blk * ATT_ROWS,
                   ATT_BAND + ATT_ROWS)
    k_keep = kh_sc[tt:tt + ATT_BAND]
    v_keep = vh_sc[tt:tt + ATT_BAND]
    kh_sc[0:ATT_BAND] = k_keep
    vh_sc[0:ATT_BAND] = v_keep


def _mixer_prompt(h, tabs, conv_w, ln_g3, ln_b3, ws, bst, bias, layer, batch, seq, tt):
    nt = seq // tt
    t_keep = (seq - ATT_BAND) // tt
    full = lambda shape: pl.BlockSpec(shape, lambda b, t: (0,) * len(shape))
    per_layer = lambda shape: pl.BlockSpec((1,) + shape, lambda b, t: (layer,) + (0,) * len(shape))
    w = ATT_BAND + tt
    assert tt == 2 * ATT_ROWS
    n_var = bias.shape[1]
    bias_spec = lambda blk: pl.BlockSpec((1, 1, ATT_HEADS, ATT_ROWS, ATT_BAND + ATT_ROWS),
                                         lambda b, t: (layer, jnp.minimum(2 * t + blk, n_var - 1), 0, 0, 0))
    return pl.pallas_call(
        functools.partial(_mixer_prompt_body, tt=tt),
        grid=(batch, nt),
        in_specs=[
            pl.BlockSpec((tt, MIX_COLS), lambda b, t: (b * nt + t, 0)),
            pl.BlockSpec((tt, 256), lambda b, t: (t, 0)),
            pl.BlockSpec((tt, 256), lambda b, t: (t, 0)),
            full((RET_HEADS, tt, tt)),
            full((tt, 256)), full((tt, 256)),
            full((2, 1, 256)),
            full((2 * RET_DK, 2 * RET_DV)),
            per_layer((CONV_W, BRANCH_W)),
            per_layer((1, BRANCH_W)), per_layer((1, BRANCH_W)),
            per_layer((GMLP_GROUPS, GMLP_BLOCK, GMLP_BLOCK)),
            per_layer((GMLP_BLOCK, GMLP_GROUPS)),
            bias_spec(0), bias_spec(1),
        ],
        out_specs=[
            pl.BlockSpec((tt, D_MODEL), lambda b, t: (b * nt + t, 0)),
            pl.BlockSpec((1, RET_HEADS, RET_DK, RET_DV), lambda b, t: (b, 0, 0, 0)),
            pl.BlockSpec((1, CONV_W - 1, BRANCH_W), lambda b, t: (b, 0, 0)),
            pl.BlockSpec((1, tt, BRANCH_W), lambda b, t: (b, jnp.maximum(t - t_keep, 0), 0)),
            pl.BlockSpec((1, tt, BRANCH_W), lambda b, t: (b, jnp.maximum(t - t_keep, 0), 0)),
        ],
        out_shape=[
            jax.ShapeDtypeStruct((batch * seq, D_MODEL), BF16),
            jax.ShapeDtypeStruct((batch, RET_HEADS, RET_DK, RET_DV), F32),
            jax.ShapeDtypeStruct((batch, CONV_W - 1, BRANCH_W), F32),
            jax.ShapeDtypeStruct((batch, ATT_BAND, BRANCH_W), F32),
            jax.ShapeDtypeStruct((batch, ATT_BAND, BRANCH_W), F32),
        ],
        scratch_shapes=[
            pltpu.VMEM((2, 2 * RET_DK, 2 * RET_DV), F32),
            pltpu.VMEM((8, BRANCH_W), F32),
            pltpu.VMEM((w, BRANCH_W), BF16),
            pltpu.VMEM((w, BRANCH_W), BF16),
        ],
        compiler_params=_cparams(2),
        name="mixer_prompt",
    )(h, tabs["cos"], tabs["sin"], tabs["dmat"], tabs["qdec"], tabs["kdec"], tabs["blk"], tabs["smask"],
      conv_w, ln_g3, ln_b3, ws, bst, bias, bias)


def _mixer_sample_body(h_ref, cos_ref, sin_ref, dmat_ref, qdec_ref, kdec_ref, blk_ref, smask_ref,
                       convw_ref, lng_ref, lnb_ref, ws_ref, bst_ref, bias_ref,
                       sret_in_ref, sconv_in_ref, ck_ref, cv_ref,
                       br_ref, sret_ref, sconv_ref, ko_ref, vo_ref, gv_ref,
                       kh_sc, vh_sc, *, length, group):
    window = kh_sc.shape[1]
    for g in range(group):
        hg = h_ref.at[g * length:(g + 1) * length]
        brg = br_ref.at[g * length:(g + 1) * length]
        kh, vh = kh_sc.at[g], vh_sc.at[g]

        states = _retention(hg, cos_ref[...], sin_ref[...], dmat_ref, qdec_ref[...], kdec_ref[...], blk_ref,
                            smask_ref[...], _state_to_pairs(sret_in_ref.at[0, g]), brg)
        _pairs_to_state(states, sret_ref.at[g])

        prev8 = jnp.concatenate([jnp.zeros((8 - (CONV_W - 1), BRANCH_W), F32), sconv_in_ref[0, g]], axis=0)
        z = _conv(hg, prev8, convw_ref[0], brg)
        sconv_ref[g] = z[length - (CONV_W - 1):length]

        gv_ref[g] = _gmlp(hg, lng_ref[0], lnb_ref[0], ws_ref, bst_ref, brg, length)

        ak = hg[:, C_AK:C_AK + BRANCH_W]
        av = hg[:, C_AV:C_AV + BRANCH_W]
        kh[0:ATT_BAND] = ck_ref[0, g]
        vh[0:ATT_BAND] = cv_ref[0, g]
        kh[ATT_BAND:ATT_BAND + length] = ak
        vh[ATT_BAND:ATT_BAND + length] = av
        pad_rows = window - (ATT_BAND + length)
        kh[ATT_BAND + length:window] = jnp.zeros((pad_rows, BRANCH_W), BF16)
        vh[ATT_BAND + length:window] = jnp.zeros((pad_rows, BRANCH_W), BF16)
        ko_ref[g] = ak.astype(F32)
        vo_ref[g] = av.astype(F32)
        _attention(hg, kh, vh, bias_ref.at[0], brg, 0, length, 0, window)


def _mixer_sample(h, tabs, conv_w, ln_g3, ln_b3, ws, bst, bias, state_ret, state_conv, cache_k, cache_v,
                  layer, streams, length, group):
    full = lambda shape: pl.BlockSpec(shape, lambda s: (0,) * len(shape))
    per_layer = lambda shape: pl.BlockSpec((1,) + shape, lambda s: (layer,) + (0,) * len(shape))
    per_stream = lambda shape: pl.BlockSpec((1, group) + shape, lambda s: (layer, s) + (0,) * len(shape))
    out_stream = lambda shape: pl.BlockSpec((group,) + shape, lambda s: (s,) + (0,) * len(shape))
    w = bias.shape[-1]
    return pl.pallas_call(
        functools.partial(_mixer_sample_body, length=length, group=group),
        grid=(streams // group,),
        in_specs=[
            pl.BlockSpec((group * length, MIX_COLS), lambda s: (s, 0)),
            full((length, 256)), full((length, 256)),
            full((RET_HEADS, length, length)),
            full((length, 256)), full((length, 256)),
            full((2, 1, 256)),
            full((2 * RET_DK, 2 * RET_DV)),
            per_layer((CONV_W, BRANCH_W)),
            per_layer((1, BRANCH_W)), per_layer((1, BRANCH_W)),
            per_layer((GMLP_GROUPS, GMLP_BLOCK, GMLP_BLOCK)),
            per_layer((GMLP_BLOCK, GMLP_GROUPS)),
            per_layer((1, ATT_HEADS, length, w)),
            per_stream((RET_HEADS, RET_DK, RET_DV)),
            per_stream((CONV_W - 1, BRANCH_W)),
            per_stream((ATT_BAND, BRANCH_W)),
            per_stream((ATT_BAND, BRANCH_W)),
        ],
        out_specs=[
            pl.BlockSpec((group * length, D_MODEL), lambda s: (s, 0)),
            out_stream((RET_HEADS, RET_DK, RET_DV)),
            out_stream((CONV_W - 1, BRANCH_W)),
            out_stream((length, BRANCH_W)),
            out_stream((length, BRANCH_W)),
            out_stream((length, BRANCH_W)),
        ],
        out_shape=[
            jax.ShapeDtypeStruct((streams * length, D_MODEL), BF16),
            jax.ShapeDtypeStruct((streams, RET_HEADS, RET_DK, RET_DV), F32),
            jax.ShapeDtypeStruct((streams, CONV_W - 1, BRANCH_W), F32),
            jax.ShapeDtypeStruct((streams, length, BRANCH_W), F32),
            jax.ShapeDtypeStruct((streams, length, BRANCH_W), F32),
            jax.ShapeDtypeStruct((streams, length, BRANCH_W), F32),
        ],
        scratch_shapes=[
            pltpu.VMEM((group, w, BRANCH_W), BF16),
            pltpu.VMEM((group, w, BRANCH_W), BF16),
        ],
        compiler_params=_cparams(1),
        name="mixer_sample",
    )(h, tabs["cos"], tabs["sin"], tabs["dmat"], tabs["qdec"], tabs["kdec"], tabs["blk"], tabs["smask"],
      conv_w, ln_g3, ln_b3, ws, bst, bias, state_ret, state_conv, cache_k, cache_v)


def _mixer_tables(positions, blk_len):
    half = RET_DK // 2
    freqs = ROPE_BASE ** (-jnp.arange(half, dtype=F32) / half)
    ang = jnp.asarray(positions, F32)[:, None] * freqs[None, :]
    cos, sin = jnp.cos(ang), jnp.sin(ang)
    cos_t = jnp.tile(cos, (1, 2 * RET_HEADS))
    sin_t = jnp.tile(jnp.concatenate([-sin, sin], axis=1), (1, RET_HEADS))
    lg = jnp.log1p(-jnp.exp2(-5.0 - jnp.arange(RET_HEADS, dtype=F32)))
    t = jnp.arange(blk_len, dtype=F32)
    diff = t[:, None] - t[None, :]
    dmat = jnp.where(diff >= 0, jnp.exp(jnp.maximum(diff, 0.0)[None] * lg[:, None, None]), 0.0)
    qdec = jnp.repeat(jnp.exp((t[:, None] + 1.0) * lg[None, :]), RET_DK, axis=1)
    kdec = jnp.repeat(jnp.exp((blk_len - 1.0 - t)[:, None] * lg[None, :]), RET_DK, axis=1)
    blk = jnp.repeat(jnp.exp(blk_len * lg), RET_DV).reshape(RET_HEADS // 2, 1, 2 * RET_DV)
    smask = jnp.kron(jnp.eye(2, dtype=F32), jnp.ones((RET_DK, RET_DV), F32))
    return dict(cos=cos_t, sin=sin_t, dmat=dmat, qdec=qdec, kdec=kdec, blk=blk, smask=smask)


def _pick_tile(n, target):
    t = min(n, target)
    while n % t:
        t //= 2
    return t


def kernel(x_prompt, x_sample, state_ret, state_conv, cache_att_k, cache_att_v, norm_gain, w_in, w_branch, w_out,
           conv_w, gmlp_ln_gain, gmlp_ln_bias, gmlp_ws, gmlp_bs, att_rel_bias, final_norm_gain):
    batch, seq, _ = x_prompt.shape
    streams, dec_len, _ = x_sample.shape
    depth = w_in.shape[0]
    assert seq % 256 == 0 and seq >= ATT_BAND and cache_att_k.shape[2] == ATT_BAND

    tt = 256
    n_p, n_s = batch * seq, streams * dec_len
    tm_p, tm_s = _pick_tile(n_p, 1024), _pick_tile(n_s, 512)

    w_mix_t, w_merge_t = _pack_weights(w_in, w_branch, w_out)
    gain3 = norm_gain.reshape(depth, 1, D_MODEL)
    ln_g3 = gmlp_ln_gain.reshape(depth, 1, BRANCH_W)
    ln_b3 = gmlp_ln_bias.reshape(depth, 1, BRANCH_W)
    bst = jnp.swapaxes(gmlp_bs, 1, 2)
    table_pad = jnp.pad(att_rel_bias, ((0, 0), (0, 0), (0, TABLE_LANES - att_rel_bias.shape[-1])))
    cache_k = cache_att_k.reshape(depth, streams, ATT_BAND, BRANCH_W).astype(BF16)
    cache_v = cache_att_v.reshape(depth, streams, ATT_BAND, BRANCH_W).astype(BF16)
    gain2 = final_norm_gain.reshape(1, D_MODEL)

    tabs_p = _mixer_tables(np.arange(seq), tt)
    tabs_s = _mixer_tables(PAST_LEN + np.arange(dec_len), dec_len)

    hp = x_prompt.reshape(n_p, D_MODEL)
    hs = x_sample.reshape(n_s, D_MODEL)
    outs = [[] for _ in range(9)]
    w_p = ATT_BAND + ATT_ROWS
    bias_p = _rel_bias(table_pad, ATT_BAND // ATT_ROWS + 1, ATT_ROWS, w_p, w_p, True)
    w_s = -(-(ATT_BAND + dec_len) // LANES) * LANES
    bias_s = _rel_bias(table_pad, 1, dec_len, w_s, ATT_BAND + dec_len, False)
    for l in range(depth):
        h_mix, xn = _inproj(hp, gain3, w_mix_t, l, tm_p, INPROJ_TN)
        br, p_ret, p_conv, p_k, p_v = _mixer_prompt(h_mix, tabs_p, conv_w, ln_g3, ln_b3, gmlp_ws, bst, bias_p,
                                                    l, batch, seq, tt)
        hp = _merge(xn, br, hp, w_merge_t, gain2, l, tm_p, l == depth - 1)

        h_mix, xn = _inproj(hs, gain3, w_mix_t, l, tm_s, INPROJ_TN)
        br, s_ret, s_conv, s_k, s_v, s_gv = _mixer_sample(h_mix, tabs_s, conv_w, ln_g3, ln_b3, gmlp_ws, bst, bias_s,
                                                          state_ret, state_conv, cache_k, cache_v,
                                                          l, streams, dec_len, _pick_tile(streams, SAMPLE_GROUP))
        hs = _merge(xn, br, hs, w_merge_t, gain2, l, tm_s, l == depth - 1)

        for dst, val in zip(outs, (p_ret, p_conv, p_k, p_v, s_ret, s_conv, s_k, s_v, s_gv)):
            dst.append(val)

    y_prompt = hp.reshape(batch, seq, D_MODEL)
    y_sample = hs.reshape(streams, dec_len, D_MODEL)
    p_ret, p_conv, p_k, p_v, s_ret, s_conv, s_k, s_v, s_gv = [jnp.stack(o) for o in outs]
    kv_shape = lambda a: a.reshape(a.shape[:3] + (ATT_HEADS, ATT_DH))
    return (y_prompt, y_sample, p_ret, p_conv, kv_shape(p_k), kv_shape(p_v),
            s_ret, s_conv, kv_shape(s_k), kv_shape(s_v), s_gv)
```

```python
import functools

import numpy as np
import jax
import jax.numpy as jnp
from jax import lax
from jax.experimental import pallas as pl
from jax.experimental.pallas import tpu as pltpu

F32 = jnp.float32
BF16 = jnp.bfloat16

D_MODEL = 2048
BRANCH_W = 512
N_BRANCH = 4
CHUNK = 64
RET_HEADS = 4
RET_DK = 64
RET_DV = 128
ROPE_BASE = 10000.0
CONV_W = 3
GMLP_BLOCK = 128
GMLP_GROUPS = 4
ATT_HEADS = 8
ATT_DH = 64
ATT_BAND = 512
MAX_REL = 128
NORM_EPS = 1e-6
PAST_LEN = 2048
NEG_BIG = -1e30

MIX_COLS = 2 * RET_HEADS * RET_DK + 13 * BRANCH_W
IN_COLS = MIX_COLS + N_BRANCH * D_MODEL
C_RQ, C_RK, C_RV, C_RG = 0, 256, 512, 1024
C_CB, C_CC, C_CX, C_CG = 1536, 2048, 2560, 3072
C_MU, C_MV, C_MG = 3584, 4096, 4608
C_AQ, C_AK, C_AV, C_AG = 5120, 5632, 6144, 6656

LANES = 128
WBLK = 256
INPROJ_TN = 7 * WBLK
ATT_ROWS = 128
MERGE_ROW_CHUNK = 256
SAMPLE_GROUP = 4
LOG2E = 1.4426950408889634
VMEM_LIMIT_BYTES = 56 * 1024 * 1024
TABLE_LANES = 384


def _cparams(n_axes):
    return pltpu.CompilerParams(dimension_semantics=("arbitrary",) * n_axes,
                                vmem_limit_bytes=VMEM_LIMIT_BYTES)


def _dot(a, b):
    return jnp.dot(a, b, preferred_element_type=F32)


def _dot_nt(a, b):
    return lax.dot_general(a, b, (((1,), (1,)), ((), ())), preferred_element_type=F32)


def _dot_tn(a, b):
    return lax.dot_general(a, b, (((0,), (0,)), ((), ())), preferred_element_type=F32)


def _silu(x):
    return x * jax.nn.sigmoid(x)


def _inproj_body(x_ref, g_ref, w_ref, h_ref, xn_ref, *, nsub):
    @pl.when(pl.program_id(1) == 0)
    def _():
        x = x_ref[...]
        r = lax.rsqrt(jnp.mean(x * x, axis=-1, keepdims=True) + NORM_EPS)
        xn_ref[...] = (x * r * g_ref[0]).astype(BF16)

    xn = xn_ref[...]
    for c in range(nsub):
        h_ref[:, c * WBLK:(c + 1) * WBLK] = _dot(xn, w_ref[0, c]).astype(h_ref.dtype)


def _inproj(x2d, gain3, w_mix_t, layer, tm, tn):
    n = x2d.shape[0]
    nsub = tn // WBLK
    return pl.pallas_call(
        functools.partial(_inproj_body, nsub=nsub),
        grid=(n // tm, MIX_COLS // tn),
        in_specs=[
            pl.BlockSpec((tm, D_MODEL), lambda i, j: (i, 0)),
            pl.BlockSpec((1, 1, D_MODEL), lambda i, j: (layer, 0, 0)),
            pl.BlockSpec((1, nsub, D_MODEL, WBLK), lambda i, j: (layer, j, 0, 0)),
        ],
        out_specs=[
            pl.BlockSpec((tm, tn), lambda i, j: (i, j)),
            pl.BlockSpec((tm, D_MODEL), lambda i, j: (i, 0)),
        ],
        out_shape=[
            jax.ShapeDtypeStruct((n, MIX_COLS), BF16),
            jax.ShapeDtypeStruct((n, D_MODEL), BF16),
        ],
        compiler_params=_cparams(2),
        name="inproj",
    )(x2d, gain3, w_mix_t)


MERGE_ROWS = N_BRANCH * D_MODEL + N_BRANCH * BRANCH_W + D_MODEL
MERGE_WO_ROW0 = N_BRANCH * D_MODEL + N_BRANCH * BRANCH_W


def _add_out_projection(hm_sc, wo_ref, out_ref):
    hm = hm_sc[...]
    for j in range(D_MODEL // WBLK):
        out_ref[:, j * WBLK:(j + 1) * WBLK] += _dot(hm, wo_ref[j * WBLK:(j + 1) * WBLK, :])


def _merge_body(xn_ref, br_ref, x_ref, w_ref, wlast_ref, *rest, final_norm):
    fg_ref, out_ref, hm_sc = rest if final_norm else (None,) + rest
    s = pl.program_id(1)
    nb = D_MODEL // WBLK
    tm = out_ref.shape[0]

    @pl.when(s == 0)
    def _():
        out_ref[...] = jnp.zeros_like(out_ref)

    @pl.when(s > 0)
    def _():
        _add_out_projection(hm_sc, w_ref.at[0, 0, MERGE_WO_ROW0:MERGE_ROWS], out_ref)

    rc = min(tm, MERGE_ROW_CHUNK)
    for r in range(tm // rc):
        rows = slice(r * rc, (r + 1) * rc)
        xn = xn_ref[rows, :]
        acc = None
        for i in range(N_BRANCH):
            logits = _dot(xn, w_ref[0, 0, i * D_MODEL:(i + 1) * D_MODEL, :])
            wb0 = N_BRANCH * D_MODEL + i * BRANCH_W
            proj = _dot(br_ref[rows, i * BRANCH_W:(i + 1) * BRANCH_W], w_ref[0, 0, wb0:wb0 + BRANCH_W, :])
            term = jax.nn.sigmoid(logits) * proj
            acc = term if acc is None else acc + term
        hm_sc[rows, :] = acc.astype(BF16)

    @pl.when(s == nb - 1)
    def _():
        _add_out_projection(hm_sc, wlast_ref.at[0, 0], out_ref)

    for c in range(nb):
        @pl.when(s == c)
        def _():
            out_ref[:, c * WBLK:(c + 1) * WBLK] += x_ref[...]

    if final_norm:
        @pl.when(s == nb - 1)
        def _():
            y = out_ref[...]
            r = lax.rsqrt(jnp.mean(y * y, axis=-1, keepdims=True) + NORM_EPS)
            out_ref[...] = y * r * fg_ref[...]


def _merge(xn, br, x2d, w_merge_t, final_gain2, layer, tm, final_norm):
    n = x2d.shape[0]
    nb = D_MODEL // WBLK
    wo_part = MERGE_WO_ROW0 // D_MODEL
    in_specs = [
        pl.BlockSpec((tm, D_MODEL), lambda m, s: (m, 0)),
        pl.BlockSpec((tm, D_MODEL), lambda m, s: (m, 0)),
        pl.BlockSpec((tm, WBLK), lambda m, s: (m, s)),
        pl.BlockSpec((1, 1, MERGE_ROWS, WBLK), lambda m, s: (layer, s, 0, 0)),
        pl.BlockSpec((1, 1, D_MODEL, WBLK), lambda m, s: (layer, 0, wo_part, 0)),
    ]
    operands = [xn, br, x2d, w_merge_t, w_merge_t]
    if final_norm:
        in_specs.append(pl.BlockSpec((1, D_MODEL), lambda m, s: (0, 0)))
        operands.append(final_gain2)
    return pl.pallas_call(
        functools.partial(_merge_body, final_norm=final_norm),
        grid=(n // tm, nb),
        in_specs=in_specs,
        out_specs=pl.BlockSpec((tm, D_MODEL), lambda m, s: (m, 0)),
        out_shape=jax.ShapeDtypeStruct((n, D_MODEL), F32),
        scratch_shapes=[pltpu.VMEM((tm, WBLK), BF16)],
        compiler_params=_cparams(2),
        name="merge",
    )(*operands)


def _pack_mix_body(w_ref, o_ref, *, nsub):
    for c in range(nsub):
        o_ref[0, c] = w_ref[0, :, c * WBLK:(c + 1) * WBLK].astype(BF16)


def _pack_merge_body(g0_ref, g1_ref, g2_ref, g3_ref, wb_ref, wo_ref, o_ref):
    for i, g_ref in enumerate((g0_ref, g1_ref, g2_ref, g3_ref)):
        o_ref[0, 0, i * D_MODEL:(i + 1) * D_MODEL, :] = g_ref[0].astype(BF16)
    r0 = N_BRANCH * D_MODEL
    o_ref[0, 0, r0:r0 + N_BRANCH * BRANCH_W, :] = wb_ref[0].astype(BF16)
    r0 += N_BRANCH * BRANCH_W
    for j in range(D_MODEL // WBLK):
        o_ref[0, 0, r0 + j * WBLK:r0 + (j + 1) * WBLK, :] = wo_ref[0, :, j * WBLK:(j + 1) * WBLK].astype(BF16)


def _pack_weights(w_in, w_branch, w_out):
    depth = w_in.shape[0]
    nb = D_MODEL // WBLK
    nsub = INPROJ_TN // WBLK
    w_mix_t = pl.pallas_call(
        functools.partial(_pack_mix_body, nsub=nsub),
        grid=(depth, MIX_COLS // INPROJ_TN),
        in_specs=[pl.BlockSpec((1, D_MODEL, INPROJ_TN), lambda l, j: (l, 0, j))],
        out_specs=pl.BlockSpec((1, nsub, D_MODEL, WBLK), lambda l, j: (l, j, 0, 0)),
        out_shape=jax.ShapeDtypeStruct((depth, MIX_COLS // WBLK, D_MODEL, WBLK), BF16),
        compiler_params=_cparams(2),
        name="pack_mix",
    )(w_in)

    gate_blk0 = MIX_COLS // WBLK
    gate_spec = lambda i: pl.BlockSpec((1, D_MODEL, WBLK), lambda l, c: (l, 0, gate_blk0 + i * nb + c))
    w_merge_t = pl.pallas_call(
        _pack_merge_body,
        grid=(depth, nb),
        in_specs=[
            gate_spec(0), gate_spec(1), gate_spec(2), gate_spec(3),
            pl.BlockSpec((1, N_BRANCH * BRANCH_W, WBLK), lambda l, c: (l, 0, c)),
            pl.BlockSpec((1, WBLK, D_MODEL), lambda l, c: (l, (c + nb - 1) % nb, 0)),
        ],
        out_specs=pl.BlockSpec((1, 1, MERGE_ROWS, WBLK), lambda l, c: (l, c, 0, 0)),
        out_shape=jax.ShapeDtypeStruct((depth, nb, MERGE_ROWS, WBLK), BF16),
        compiler_params=_cparams(2),
        name="pack_merge",
    )(w_in, w_in, w_in, w_in, w_branch.reshape(depth, N_BRANCH * BRANCH_W, D_MODEL), w_out)
    return w_mix_t, w_merge_t


def _bias_body(tab_ref, perm_ref, out_ref, *, n_var, tq, w, w_valid, wpad, banded):
    t = tab_ref[0] * LOG2E
    perm = perm_ref[...]
    t_hi = t.astype(BF16)
    r1 = t - t_hi.astype(F32)
    t_mid = r1.astype(BF16)
    t_lo = (r1 - t_mid.astype(F32)).astype(BF16)
    row0 = _dot(t_hi, perm) + _dot(t_mid, perm) + _dot(t_lo, perm)
    qi = lax.broadcasted_iota(jnp.int32, (tq, w), 0)
    kj = lax.broadcasted_iota(jnp.int32, (tq, w), 1)
    visible = kj < w_valid
    if banded:
        lo = (qi // CHUNK) * CHUNK
        visible = visible & (kj >= lo) & (kj < lo + ATT_BAND + CHUNK)
    for h in range(ATT_HEADS):
        full = jnp.broadcast_to(row0[h:h + 1, :], (tq, wpad))
        shifted = pltpu.roll(full, 0, 1, stride=1, stride_axis=0)[:, :w]
        for u in range(n_var):
            vis_u = visible & (kj >= ATT_BAND - u * tq) if banded else visible
            out_ref[0, u, h] = jnp.where(vis_u, shifted, NEG_BIG)


def _bias_perm(w, wpad):
    j = np.arange(wpad)
    m = np.clip(ATT_BAND - j, -MAX_REL, MAX_REL) + MAX_REL
    m = np.where(j >= w, 2 * MAX_REL, m)
    perm = np.zeros((TABLE_LANES, wpad), np.float32)
    perm[m, j] = 1.0
    return jnp.asarray(perm, BF16)


def _rel_bias(table_pad, n_var, tq, w, w_valid, banded):
    depth = table_pad.shape[0]
    wpad = -(-(w + tq) // LANES) * LANES
    return pl.pallas_call(
        functools.partial(_bias_body, n_var=n_var, tq=tq, w=w, w_valid=w_valid, wpad=wpad, banded=banded),
        grid=(depth,),
        in_specs=[
            pl.BlockSpec((1, ATT_HEADS, TABLE_LANES), lambda l: (l, 0, 0)),
            pl.BlockSpec((TABLE_LANES, wpad), lambda l: (0, 0)),
        ],
        out_specs=pl.BlockSpec((1, n_var, ATT_HEADS, tq, w), lambda l: (l, 0, 0, 0, 0)),
        out_shape=jax.ShapeDtypeStruct((depth, n_var, ATT_HEADS, tq, w), F32),
        compiler_params=_cparams(1),
        name="rel_bias",
    )(table_pad, _bias_perm(w, wpad))


def _rotary(x, cos, sin_signed):
    lane = lax.broadcasted_iota(jnp.int32, x.shape, 1)
    first_half = (lane % RET_DK) < (RET_DK // 2)
    width = x.shape[1]
    swapped = jnp.where(first_half, pltpu.roll(x, width - RET_DK // 2, 1), pltpu.roll(x, RET_DK // 2, 1))
    return x * cos + swapped * sin_signed


def _retention(h_ref, cos, sin_signed, dmat_ref, qdec, kdec, blk_ref, smask, states, br_ref):
    length = h_ref.shape[0]
    q = _rotary(h_ref[:, C_RQ:C_RQ + 256].astype(F32), cos, sin_signed)
    k = _rotary(h_ref[:, C_RK:C_RK + 256].astype(F32), cos, sin_signed) * (RET_DK ** -0.5)
    kb = k.astype(BF16)
    qd, kd = (q * qdec).astype(BF16), (k * kdec).astype(BF16)
    lane = lax.broadcasted_iota(jnp.int32, (length, LANES), 1)
    new_states = []
    for p in range(RET_HEADS // 2):
        cols = slice(p * LANES, (p + 1) * LANES)
        sp = states[p]
        o_inter = _dot(qd[:, cols], sp.astype(BF16))
        for half in range(2):
            hh = 2 * p + half
            in_head = (lane >= half * RET_DK) & (lane < (half + 1) * RET_DK)
            qm = jnp.where(in_head, q[:, cols], 0.0).astype(BF16)
            inner = (_dot_nt(qm, kb[:, cols]) * dmat_ref[hh]).astype(BF16)
            vcols = slice(C_RV + hh * RET_DV, C_RV + (hh + 1) * RET_DV)
            o = _dot(inner, h_ref[:, vcols]) + o_inter[:, half * RET_DV:(half + 1) * RET_DV]
            mu = jnp.mean(o, axis=-1, keepdims=True)
            oc = o - mu
            var = jnp.mean(oc * oc, axis=-1, keepdims=True)
            hn = oc * lax.rsqrt(var + NORM_EPS)
            gcols = slice(C_RG + hh * RET_DV, C_RG + (hh + 1) * RET_DV)
            br_ref[:, hh * RET_DV:(hh + 1) * RET_DV] = (_silu(h_ref[:, gcols].astype(F32)) * hn).astype(BF16)
        v_pair = h_ref[:, C_RV + p * 2 * RET_DV:C_RV + (p + 1) * 2 * RET_DV]
        upd = _dot_tn(kd[:, cols], v_pair)
        new_states.append((sp * blk_ref[p] + upd) * smask)
    return new_states


def _conv(h_ref, prev8, convw, br_ref):
    z = h_ref[:, C_CC:C_CC + BRANCH_W].astype(F32) * h_ref[:, C_CX:C_CX + BRANCH_W].astype(F32)
    zcat = jnp.concatenate([prev8, z], axis=0)
    z1 = pltpu.roll(zcat, 1, 0)[8:]
    z2 = pltpu.roll(zcat, 2, 0)[8:]
    y = convw[0:1] * z2 + convw[1:2] * z1 + convw[2:3] * z
    cb = h_ref[:, C_CB:C_CB + BRANCH_W].astype(F32)
    cg = h_ref[:, C_CG:C_CG + BRANCH_W].astype(F32)
    br_ref[:, BRANCH_W:2 * BRANCH_W] = (_silu(cg) * (cb * y)).astype(BF16)
    return z


def _gmlp(h_ref, lng, lnb, ws_ref, bst_ref, br_ref, blk_len):
    length = h_ref.shape[0]
    mv = h_ref[:, C_MV:C_MV + BRANCH_W].astype(F32)
    mean = jnp.mean(mv, axis=-1, keepdims=True)
    cen = mv - mean
    var = jnp.mean(cen * cen, axis=-1, keepdims=True)
    vn = cen * lax.rsqrt(var + NORM_EPS) * lng + lnb
    vnb = vn.astype(BF16)
    ri = lax.broadcasted_iota(jnp.int32, (blk_len, blk_len), 0)
    ci = lax.broadcasted_iota(jnp.int32, (blk_len, blk_len), 1)
    causal = (ci // CHUNK) <= (ri // CHUNK)
    for g in range(GMLP_GROUPS):
        wg = jnp.where(causal, ws_ref[0, g, 0:blk_len, 0:blk_len], 0.0).astype(BF16)
        bcol = bst_ref[0, 0:blk_len, g:g + 1]
        for n in range(length // blk_len):
            rows = slice(n * blk_len, (n + 1) * blk_len)
            gc = slice(g * LANES, (g + 1) * LANES)
            mix = _dot(wg, vnb[rows, gc]) + bcol
            mu = h_ref[rows, C_MU + g * LANES:C_MU + (g + 1) * LANES].astype(F32)
            mg = h_ref[rows, C_MG + g * LANES:C_MG + (g + 1) * LANES].astype(F32)
            br_ref[rows, 2 * BRANCH_W + g * LANES:2 * BRANCH_W + (g + 1) * LANES] = (
                _silu(mg) * (mu * mix)).astype(BF16)
    return vn


def _attention(h_ref, kh_ref, vh_ref, bias_ref, br_ref, r0, rows, c0, width):
    lane = lax.broadcasted_iota(jnp.int32, (rows, LANES), 1)
    rsl = slice(r0, r0 + rows)
    for p in range(ATT_HEADS // 2):
        cols = slice(p * LANES, (p + 1) * LANES)
        qp = h_ref[rsl, C_AQ + p * LANES:C_AQ + (p + 1) * LANES].astype(F32) * (LOG2E * ATT_DH ** -0.5)
        kp = kh_ref[c0:c0 + width, cols]
        vp = vh_ref[c0:c0 + width, cols]
        first = lane < ATT_DH
        q2 = jnp.concatenate([jnp.where(first, qp, 0.0), jnp.where(first, 0.0, qp)], axis=0).astype(BF16)
        s = _dot_nt(q2, kp) + bias_ref[0, 2 * p:2 * p + 2].reshape(2 * rows, width)
        m = jnp.max(s, axis=-1, keepdims=True)
        e = jnp.exp2(s - m)
        denom = jnp.sum(e, axis=-1, keepdims=True)
        o = _dot(e.astype(BF16), vp) / denom
        acc = jnp.where(first, o[0:rows], o[rows:2 * rows])
        ag = h_ref[rsl, C_AG + p * LANES:C_AG + (p + 1) * LANES].astype(F32)
        br_ref[rsl, 3 * BRANCH_W + p * LANES:3 * BRANCH_W + (p + 1) * LANES] = (_silu(ag) * acc).astype(BF16)


def _attention_sample(h_ref, kt, vt, ak, av, bias_ref, br_ref):
    length = h_ref.shape[0]
    n_old = kt.shape[1]
    lane_head = lax.broadcasted_iota(jnp.int32, (length, BRANCH_W), 1) // ATT_DH
    q = h_ref[:, C_AQ:C_AQ + BRANCH_W].astype(F32) * (LOG2E * ATT_DH ** -0.5)
    q_st = jnp.concatenate([jnp.where(lane_head == hh, q, 0.0) for hh in range(ATT_HEADS)], axis=0).astype(BF16)
    bias = bias_ref[0, 0].reshape(ATT_HEADS * length, bias_ref.shape[-1])
    s_old = _dot(q_st, kt) + bias[:, 0:n_old]
    s_new = _dot_nt(q_st, ak) + bias[:, n_old:n_old + length]
    m = jnp.maximum(jnp.max(s_old, axis=-1, keepdims=True), jnp.max(s_new, axis=-1, keepdims=True))
    e_old = jnp.exp2(s_old - m)
    e_new = jnp.exp2(s_new - m)
    denom = jnp.sum(e_old, axis=-1, keepdims=True) + jnp.sum(e_new, axis=-1, keepdims=True)
    o_st = (_dot_nt(e_old.astype(BF16), vt) + _dot(e_new.astype(BF16), av)) / denom
    o = None
    for hh in range(ATT_HEADS):
        part = jnp.where(lane_head == hh, o_st[hh * length:(hh + 1) * length], 0.0)
        o = part if o is None else o + part
    ag = h_ref[:, C_AG:C_AG + BRANCH_W].astype(F32)
    br_ref[:, 3 * BRANCH_W:4 * BRANCH_W] = (_silu(ag) * o).astype(BF16)


def _state_to_pairs(state_ref):
    pairs = []
    zero = jnp.zeros((RET_DK, RET_DV), F32)
    for p in range(RET_HEADS // 2):
        top = jnp.concatenate([state_ref[2 * p], zero], axis=1)
        bot = jnp.concatenate([zero, state_ref[2 * p + 1]], axis=1)
        pairs.append(jnp.concatenate([top, bot], axis=0))
    return pairs


def _pairs_to_state(pairs, out_ref):
    for p in range(RET_HEADS // 2):
        out_ref[2 * p] = pairs[p][0:RET_DK, 0:RET_DV]
        out_ref[2 * p + 1] = pairs[p][RET_DK:2 * RET_DK, RET_DV:2 * RET_DV]


def _mixer_prompt_body(h_ref, cos_ref, sin_ref, dmat_ref, qdec_ref, kdec_ref, blk_ref, smask_ref,
                       convw_ref, lng_ref, lnb_ref, ws_ref, bst_ref, bias0_ref, bias1_ref,
                       br_ref, sret_ref, sconv_ref, ko_ref, vo_ref,
                       sp_sc, zc_sc, kh_sc, vh_sc, *, tt, t_keep):
    t = pl.program_id(1)

    @pl.when(t == 0)
    def _():
        sp_sc[...] = jnp.zeros_like(sp_sc)
        zc_sc[...] = jnp.zeros_like(zc_sc)
        kh_sc[0:ATT_BAND] = jnp.zeros((ATT_BAND, BRANCH_W), BF16)
        vh_sc[0:ATT_BAND] = jnp.zeros((ATT_BAND, BRANCH_W), BF16)

    states = _retention(h_ref, cos_ref[...], sin_ref[...], dmat_ref, qdec_ref[...], kdec_ref[...], blk_ref,
                        smask_ref[...], [sp_sc[0], sp_sc[1]], br_ref)
    sp_sc[0] = states[0]
    sp_sc[1] = states[1]
    _pairs_to_state(states, sret_ref.at[0])

    z = _conv(h_ref, zc_sc[...], convw_ref[0], br_ref)
    tail = z[tt - (CONV_W - 1):tt]
    zc_sc[8 - (CONV_W - 1):8] = tail
    sconv_ref[0] = tail

    _gmlp(h_ref, lng_ref[0], lnb_ref[0], ws_ref, bst_ref, br_ref, GMLP_BLOCK)

    ak = h_ref[:, C_AK:C_AK + BRANCH_W]
    av = h_ref[:, C_AV:C_AV + BRANCH_W]
    kh_sc[ATT_BAND:ATT_BAND + tt] = ak
    vh_sc[ATT_BAND:ATT_BAND + tt] = av
    for blk, bias_ref in enumerate((bias0_ref, bias1_ref)):
        _attention(h_ref, kh_sc, vh_sc, bias_ref.at[0], br_ref, blk * ATT_ROWS, ATT_ROWS, blk * ATT_ROWS,
                   ATT_BAND + ATT_ROWS)
    k_keep = kh_sc[tt:tt + ATT_BAND]
    v_keep = vh_sc[tt:tt + ATT_BAND]
    kh_sc[0:ATT_BAND] = k_keep
    vh_sc[0:ATT_BAND] = v_keep

    @pl.when(t >= t_keep)
    def _():
        ko_ref[0] = ak.astype(F32).T
        vo_ref[0] = av.astype(F32).T


def _mixer_prompt(h, tabs, conv_w, ln_g3, ln_b3, ws, bst, bias, layer, batch, seq, tt):
    nt = seq // tt
    t_keep = (seq - ATT_BAND) // tt
    full = lambda shape: pl.BlockSpec(shape, lambda b, t: (0,) * len(shape))
    per_layer = lambda shape: pl.BlockSpec((1,) + shape, lambda b, t: (layer,) + (0,) * len(shape))
    w = ATT_BAND + tt
    assert tt == 2 * ATT_ROWS
    n_var = bias.shape[1]
    bias_spec = lambda blk: pl.BlockSpec((1, 1, ATT_HEADS, ATT_ROWS, ATT_BAND + ATT_ROWS),
                                         lambda b, t: (layer, jnp.minimum(2 * t + blk, n_var - 1), 0, 0, 0))
    return pl.pallas_call(
        functools.partial(_mixer_prompt_body, tt=tt, t_keep=t_keep),
        grid=(batch, nt),
        in_specs=[
            pl.BlockSpec((tt, MIX_COLS), lambda b, t: (b * nt + t, 0)),
            pl.BlockSpec((tt, 256), lambda b, t: (t, 0)),
            pl.BlockSpec((tt, 256), lambda b, t: (t, 0)),
            full((RET_HEADS, tt, tt)),
            full((tt, 256)), full((tt, 256)),
            full((2, 1, 256)),
            full((2 * RET_DK, 2 * RET_DV)),
            per_layer((CONV_W, BRANCH_W)),
            per_layer((1, BRANCH_W)), per_layer((1, BRANCH_W)),
            per_layer((GMLP_GROUPS, GMLP_BLOCK, GMLP_BLOCK)),
            per_layer((GMLP_BLOCK, GMLP_GROUPS)),
            bias_spec(0), bias_spec(1),
        ],
        out_specs=[
            pl.BlockSpec((tt, D_MODEL), lambda b, t: (b * nt + t, 0)),
            pl.BlockSpec((1, RET_HEADS, RET_DK, RET_DV), lambda b, t: (b, 0, 0, 0)),
            pl.BlockSpec((1, CONV_W - 1, BRANCH_W), lambda b, t: (b, 0, 0)),
            pl.BlockSpec((1, BRANCH_W, tt), lambda b, t: (b, 0, jnp.maximum(t - t_keep, 0))),
            pl.BlockSpec((1, BRANCH_W, tt), lambda b, t: (b, 0, jnp.maximum(t - t_keep, 0))),
        ],
        out_shape=[
            jax.ShapeDtypeStruct((batch * seq, D_MODEL), BF16),
            jax.ShapeDtypeStruct((batch, RET_HEADS, RET_DK, RET_DV), F32),
            jax.ShapeDtypeStruct((batch, CONV_W - 1, BRANCH_W), F32),
            jax.ShapeDtypeStruct((batch, BRANCH_W, ATT_BAND), F32),
            jax.ShapeDtypeStruct((batch, BRANCH_W, ATT_BAND), F32),
        ],
        scratch_shapes=[
            pltpu.VMEM((2, 2 * RET_DK, 2 * RET_DV), F32),
            pltpu.VMEM((8, BRANCH_W), F32),
            pltpu.VMEM((w, BRANCH_W), BF16),
            pltpu.VMEM((w, BRANCH_W), BF16),
        ],
        compiler_params=_cparams(2),
        name="mixer_prompt",
    )(h, tabs["cos"], tabs["sin"], tabs["dmat"], tabs["qdec"], tabs["kdec"], tabs["blk"], tabs["smask"],
      conv_w, ln_g3, ln_b3, ws, bst, bias, bias)


def _mixer_sample_body(h_ref, cos_ref, sin_ref, dmat_ref, qdec_ref, kdec_ref, blk_ref, smask_ref,
                       convw_ref, lng_ref, lnb_ref, ws_ref, bst_ref, bias_ref,
                       sret_in_ref, sconv_in_ref, ck_ref, cv_ref,
                       br_ref, sret_ref, sconv_ref, ko_ref, vo_ref, gv_ref, *, length, group):
    for g in range(group):
        hg = h_ref.at[g * length:(g + 1) * length]
        brg = br_ref.at[g * length:(g + 1) * length]

        states = _retention(hg, cos_ref[...], sin_ref[...], dmat_ref, qdec_ref[...], kdec_ref[...], blk_ref,
                            smask_ref[...], _state_to_pairs(sret_in_ref.at[0, g]), brg)
        _pairs_to_state(states, sret_ref.at[g])

        prev8 = jnp.concatenate([jnp.zeros((8 - (CONV_W - 1), BRANCH_W), F32), sconv_in_ref[0, g]], axis=0)
        z = _conv(hg, prev8, convw_ref[0], brg)
        sconv_ref[g] = z[length - (CONV_W - 1):length]

        gv_ref[g] = _gmlp(hg, lng_ref[0], lnb_ref[0], ws_ref, bst_ref, brg, length)

        ak = hg[:, C_AK:C_AK + BRANCH_W]
        av = hg[:, C_AV:C_AV + BRANCH_W]
        ko_ref[g] = ak.astype(F32)
        vo_ref[g] = av.astype(F32)
        _attention_sample(hg, ck_ref[0, g].astype(BF16), cv_ref[0, g].astype(BF16), ak, av, bias_ref, brg)


def _mixer_sample(h, tabs, conv_w, ln_g3, ln_b3, ws, bst, bias, state_ret, state_conv, cache_k, cache_v,
                  layer, streams, length, group):
    full = lambda shape: pl.BlockSpec(shape, lambda s: (0,) * len(shape))
    per_layer = lambda shape: pl.BlockSpec((1,) + shape, lambda s: (layer,) + (0,) * len(shape))
    per_stream = lambda shape: pl.BlockSpec((1, group) + shape, lambda s: (layer, s) + (0,) * len(shape))
    out_stream = lambda shape: pl.BlockSpec((group,) + shape, lambda s: (s,) + (0,) * len(shape))
    w = bias.shape[-1]
    return pl.pallas_call(
        functools.partial(_mixer_sample_body, length=length, group=group),
        grid=(streams // group,),
        in_specs=[
            pl.BlockSpec((group * length, MIX_COLS), lambda s: (s, 0)),
            full((length, 256)), full((length, 256)),
            full((RET_HEADS, length, length)),
            full((length, 256)), full((length, 256)),
            full((2, 1, 256)),
            full((2 * RET_DK, 2 * RET_DV)),
            per_layer((CONV_W, BRANCH_W)),
            per_layer((1, BRANCH_W)), per_layer((1, BRANCH_W)),
            per_layer((GMLP_GROUPS, GMLP_BLOCK, GMLP_BLOCK)),
            per_layer((GMLP_BLOCK, GMLP_GROUPS)),
            per_layer((1, ATT_HEADS, length, w)),
            per_stream((RET_HEADS, RET_DK, RET_DV)),
            per_stream((CONV_W - 1, BRANCH_W)),
            per_stream((BRANCH_W, ATT_BAND)),
            per_stream((BRANCH_W, ATT_BAND)),
        ],
        out_specs=[
            pl.BlockSpec((group * length, D_MODEL), lambda s: (s, 0)),
            out_stream((RET_HEADS, RET_DK, RET_DV)),
            out_stream((CONV_W - 1, BRANCH_W)),
            out_stream((length, BRANCH_W)),
            out_stream((length, BRANCH_W)),
            out_stream((length, BRANCH_W)),
        ],
        out_shape=[
            jax.ShapeDtypeStruct((streams * length, D_MODEL), BF16),
            jax.ShapeDtypeStruct((streams, RET_HEADS, RET_DK, RET_DV), F32),
            jax.ShapeDtypeStruct((streams, CONV_W - 1, BRANCH_W), F32),
            jax.ShapeDtypeStruct((streams, length, BRANCH_W), F32),
            jax.ShapeDtypeStruct((streams, length, BRANCH_W), F32),
            jax.ShapeDtypeStruct((streams, length, BRANCH_W), F32),
        ],
        compiler_params=_cparams(1),
        name="mixer_sample",
    )(h, tabs["cos"], tabs["sin"], tabs["dmat"], tabs["qdec"], tabs["kdec"], tabs["blk"], tabs["smask"],
      conv_w, ln_g3, ln_b3, ws, bst, bias, state_ret, state_conv, cache_k, cache_v)


def _mixer_tables(positions, blk_len):
    half = RET_DK // 2
    freqs = ROPE_BASE ** (-jnp.arange(half, dtype=F32) / half)
    ang = jnp.asarray(positions, F32)[:, None] * freqs[None, :]
    cos, sin = jnp.cos(ang), jnp.sin(ang)
    cos_t = jnp.tile(cos, (1, 2 * RET_HEADS))
    sin_t = jnp.tile(jnp.concatenate([-sin, sin], axis=1), (1, RET_HEADS))
    lg = jnp.log1p(-jnp.exp2(-5.0 - jnp.arange(RET_HEADS, dtype=F32)))
    t = jnp.arange(blk_len, dtype=F32)
    diff = t[:, None] - t[None, :]
    dmat = jnp.where(diff >= 0, jnp.exp(jnp.maximum(diff, 0.0)[None] * lg[:, None, None]), 0.0)
    qdec = jnp.repeat(jnp.exp((t[:, None] + 1.0) * lg[None, :]), RET_DK, axis=1)
    kdec = jnp.repeat(jnp.exp((blk_len - 1.0 - t)[:, None] * lg[None, :]), RET_DK, axis=1)
    blk = jnp.repeat(jnp.exp(blk_len * lg), RET_DV).reshape(RET_HEADS // 2, 1, 2 * RET_DV)
    smask = jnp.kron(jnp.eye(2, dtype=F32), jnp.ones((RET_DK, RET_DV), F32))
    return dict(cos=cos_t, sin=sin_t, dmat=dmat, qdec=qdec, kdec=kdec, blk=blk, smask=smask)


def _pick_tile(n, target):
    t = min(n, target)
    while n % t:
        t //= 2
    return t


def kernel(x_prompt, x_sample, state_ret, state_conv, cache_att_k, cache_att_v, norm_gain, w_in, w_branch, w_out,
           conv_w, gmlp_ln_gain, gmlp_ln_bias, gmlp_ws, gmlp_bs, att_rel_bias, final_norm_gain):
    batch, seq, _ = x_prompt.shape
    streams, dec_len, _ = x_sample.shape
    depth = w_in.shape[0]
    assert seq % 256 == 0 and seq >= ATT_BAND and cache_att_k.shape[2] == ATT_BAND

    tt = 256
    n_p, n_s = batch * seq, streams * dec_len
    tm_p, tm_s = _pick_tile(n_p, 1024), _pick_tile(n_s, 512)

    w_mix_t, w_merge_t = _pack_weights(w_in, w_branch, w_out)
    gain3 = norm_gain.reshape(depth, 1, D_MODEL)
    ln_g3 = gmlp_ln_gain.reshape(depth, 1, BRANCH_W)
    ln_b3 = gmlp_ln_bias.reshape(depth, 1, BRANCH_W)
    bst = jnp.swapaxes(gmlp_bs, 1, 2)
    table_pad = jnp.pad(att_rel_bias, ((0, 0), (0, 0), (0, TABLE_LANES - att_rel_bias.shape[-1])))
    to_t = lambda c: jnp.transpose(c, (0, 1, 3, 4, 2)).reshape(depth, streams, BRANCH_W, ATT_BAND)
    cache_k, cache_v = to_t(cache_att_k), to_t(cache_att_v)
    gain2 = final_norm_gain.reshape(1, D_MODEL)

    tabs_p = _mixer_tables(np.arange(seq), tt)
    tabs_s = _mixer_tables(PAST_LEN + np.arange(dec_len), dec_len)

    hp = x_prompt.reshape(n_p, D_MODEL)
    hs = x_sample.reshape(n_s, D_MODEL)
    outs = [[] for _ in range(9)]
    w_p = ATT_BAND + ATT_ROWS
    bias_p = _rel_bias(table_pad, ATT_BAND // ATT_ROWS + 1, ATT_ROWS, w_p, w_p, True)
    w_s = -(-(ATT_BAND + dec_len) // LANES) * LANES
    bias_s = _rel_bias(table_pad, 1, dec_len, w_s, ATT_BAND + dec_len, False)
    for l in range(depth):
        h_mix, xn = _inproj(hp, gain3, w_mix_t, l, tm_p, INPROJ_TN)
        br, p_ret, p_conv, p_k, p_v = _mixer_prompt(h_mix, tabs_p, conv_w, ln_g3, ln_b3, gmlp_ws, bst, bias_p,
                                                    l, batch, seq, tt)
        hp = _merge(xn, br, hp, w_merge_t, gain2, l, tm_p, l == depth - 1)

        h_mix, xn = _inproj(hs, gain3, w_mix_t, l, tm_s, INPROJ_TN)
        br, s_ret, s_conv, s_k, s_v, s_gv = _mixer_sample(h_mix, tabs_s, conv_w, ln_g3, ln_b3, gmlp_ws, bst, bias_s,
                                                          state_ret, state_conv, cache_k, cache_v,
                                                          l, streams, dec_len, _pick_tile(streams, SAMPLE_GROUP))
        hs = _merge(xn, br, hs, w_merge_t, gain2, l, tm_s, l == depth - 1)

        for dst, val in zip(outs, (p_ret, p_conv, p_k, p_v, s_ret, s_conv, s_k, s_v, s_gv)):
            dst.append(val)

    y_prompt = hp.reshape(batch, seq, D_MODEL)
    y_sample = hs.reshape(streams, dec_len, D_MODEL)
    p_ret, p_conv, p_k, p_v, s_ret, s_conv, s_k, s_v, s_gv = [jnp.stack(o) for o in outs]
    kv_shape = lambda a: a.reshape(a.shape[:3] + (ATT_HEADS, ATT_DH))
    kv_from_t = lambda a: jnp.transpose(a.reshape(a.shape[:2] + (ATT_HEADS, ATT_DH, ATT_BAND)), (0, 1, 4, 2, 3))
    return (y_prompt, y_sample, p_ret, p_conv, kv_from_t(p_k), kv_from_t(p_v),
            s_ret, s_conv, kv_shape(s_k), kv_shape(s_v), s_gv)
```

```python
import functools

import numpy as np
import jax
import jax.numpy as jnp
from jax import lax
from jax.experimental import pallas as pl
from jax.experimental.pallas import tpu as pltpu

F32 = jnp.float32
BF16 = jnp.bfloat16

D_MODEL = 2048
BRANCH_W = 512
N_BRANCH = 4
CHUNK = 64
RET_HEADS = 4
RET_DK = 64
RET_DV = 128
ROPE_BASE = 10000.0
CONV_W = 3
GMLP_BLOCK = 128
GMLP_GROUPS = 4
ATT_HEADS = 8
ATT_DH = 64
ATT_BAND = 512
MAX_REL = 128
NORM_EPS = 1e-6
PAST_LEN = 2048
NEG_BIG = -1e30

MIX_COLS = 2 * RET_HEADS * RET_DK + 13 * BRANCH_W
IN_COLS = MIX_COLS + N_BRANCH * D_MODEL
C_RQ, C_RK, C_RV, C_RG = 0, 256, 512, 1024
C_CB, C_CC, C_CX, C_CG = 1536, 2048, 2560, 3072
C_MU, C_MV, C_MG = 3584, 4096, 4608
C_AQ, C_AK, C_AV, C_AG = 5120, 5632, 6144, 6656

LANES = 128
WBLK = 256
INPROJ_TN = 7 * WBLK
ATT_ROWS = 256
MIXER_TT = 256
MERGE_ROW_CHUNK = 256
SAMPLE_GROUP = 4
LOG2E = 1.4426950408889634
VMEM_LIMIT_BYTES = 56 * 1024 * 1024
TABLE_LANES = 384


def _cparams(n_axes):
    return pltpu.CompilerParams(dimension_semantics=("arbitrary",) * n_axes,
                                vmem_limit_bytes=VMEM_LIMIT_BYTES)


def _dot(a, b):
    return jnp.dot(a, b, preferred_element_type=F32)


def _dot_nt(a, b):
    return lax.dot_general(a, b, (((1,), (1,)), ((), ())), preferred_element_type=F32)


def _dot_tn(a, b):
    return lax.dot_general(a, b, (((0,), (0,)), ((), ())), preferred_element_type=F32)


def _silu(x):
    return x * jax.nn.sigmoid(x)


def _inproj_body(x_ref, g_ref, w_ref, h_ref, xn_ref, *, nsub):
    @pl.when(pl.program_id(1) == 0)
    def _():
        x = x_ref[...]
        r = lax.rsqrt(jnp.mean(x * x, axis=-1, keepdims=True) + NORM_EPS)
        xn_ref[...] = (x * r * g_ref[0]).astype(BF16)

    xn = xn_ref[...]
    for c in range(nsub):
        h_ref[:, c * WBLK:(c + 1) * WBLK] = _dot(xn, w_ref[0, c]).astype(h_ref.dtype)


def _inproj(x2d, gain3, w_mix_t, layer, tm, tn):
    n = x2d.shape[0]
    nsub = tn // WBLK
    return pl.pallas_call(
        functools.partial(_inproj_body, nsub=nsub),
        grid=(n // tm, MIX_COLS // tn),
        in_specs=[
            pl.BlockSpec((tm, D_MODEL), lambda i, j: (i, 0)),
            pl.BlockSpec((1, 1, D_MODEL), lambda i, j: (layer, 0, 0)),
            pl.BlockSpec((1, nsub, D_MODEL, WBLK), lambda i, j: (layer, j, 0, 0)),
        ],
        out_specs=[
            pl.BlockSpec((tm, tn), lambda i, j: (i, j)),
            pl.BlockSpec((tm, D_MODEL), lambda i, j: (i, 0)),
        ],
        out_shape=[
            jax.ShapeDtypeStruct((n, MIX_COLS), BF16),
            jax.ShapeDtypeStruct((n, D_MODEL), BF16),
        ],
        compiler_params=_cparams(2),
        name="inproj",
    )(x2d, gain3, w_mix_t)


MERGE_ROWS = N_BRANCH * D_MODEL + N_BRANCH * BRANCH_W + D_MODEL
MERGE_WO_ROW0 = N_BRANCH * D_MODEL + N_BRANCH * BRANCH_W


def _add_out_projection(hm_sc, wo_ref, out_ref):
    hm = hm_sc[...]
    for j in range(D_MODEL // WBLK):
        out_ref[:, j * WBLK:(j + 1) * WBLK] += _dot(hm, wo_ref[j * WBLK:(j + 1) * WBLK, :])


def _merge_body(xn_ref, br_ref, x_ref, w_ref, wlast_ref, *rest, final_norm):
    fg_ref, out_ref, hm_sc = rest if final_norm else (None,) + rest
    s = pl.program_id(1)
    nb = D_MODEL // WBLK
    tm = out_ref.shape[0]

    @pl.when(s == 0)
    def _():
        out_ref[...] = jnp.zeros_like(out_ref)

    @pl.when(s > 0)
    def _():
        _add_out_projection(hm_sc, w_ref.at[0, 0, MERGE_WO_ROW0:MERGE_ROWS], out_ref)

    rc = min(tm, MERGE_ROW_CHUNK)
    for r in range(tm // rc):
        rows = slice(r * rc, (r + 1) * rc)
        xn = xn_ref[rows, :]
        acc = None
        for i in range(N_BRANCH):
            logits = _dot(xn, w_ref[0, 0, i * D_MODEL:(i + 1) * D_MODEL, :])
            wb0 = N_BRANCH * D_MODEL + i * BRANCH_W
            proj = _dot(br_ref[rows, i * BRANCH_W:(i + 1) * BRANCH_W], w_ref[0, 0, wb0:wb0 + BRANCH_W, :])
            term = jax.nn.sigmoid(logits) * proj
            acc = term if acc is None else acc + term
        hm_sc[rows, :] = acc.astype(BF16)

    @pl.when(s == nb - 1)
    def _():
        _add_out_projection(hm_sc, wlast_ref.at[0, 0], out_ref)

    for c in range(nb):
        @pl.when(s == c)
        def _():
            out_ref[:, c * WBLK:(c + 1) * WBLK] += x_ref[...]

    if final_norm:
        @pl.when(s == nb - 1)
        def _():
            y = out_ref[...]
            r = lax.rsqrt(jnp.mean(y * y, axis=-1, keepdims=True) + NORM_EPS)
            out_ref[...] = y * r * fg_ref[...]


def _merge(xn, br, x2d, w_merge_t, final_gain2, layer, tm, final_norm):
    n = x2d.shape[0]
    nb = D_MODEL // WBLK
    wo_part = MERGE_WO_ROW0 // D_MODEL
    in_specs = [
        pl.BlockSpec((tm, D_MODEL), lambda m, s: (m, 0)),
        pl.BlockSpec((tm, D_MODEL), lambda m, s: (m, 0)),
        pl.BlockSpec((tm, WBLK), lambda m, s: (m, s)),
        pl.BlockSpec((1, 1, MERGE_ROWS, WBLK), lambda m, s: (layer, s, 0, 0)),
        pl.BlockSpec((1, 1, D_MODEL, WBLK), lambda m, s: (layer, 0, wo_part, 0)),
    ]
    operands = [xn, br, x2d, w_merge_t, w_merge_t]
    if final_norm:
        in_specs.append(pl.BlockSpec((1, D_MODEL), lambda m, s: (0, 0)))
        operands.append(final_gain2)
    return pl.pallas_call(
        functools.partial(_merge_body, final_norm=final_norm),
        grid=(n // tm, nb),
        in_specs=in_specs,
        out_specs=pl.BlockSpec((tm, D_MODEL), lambda m, s: (m, 0)),
        out_shape=jax.ShapeDtypeStruct((n, D_MODEL), F32),
        scratch_shapes=[pltpu.VMEM((tm, WBLK), BF16)],
        compiler_params=_cparams(2),
        name="merge",
    )(*operands)


def _pack_mix_body(w_ref, o_ref, *, nsub):
    for c in range(nsub):
        o_ref[0, c] = w_ref[0, :, c * WBLK:(c + 1) * WBLK].astype(BF16)


def _pack_merge_body(g0_ref, g1_ref, g2_ref, g3_ref, wb_ref, wo_ref, o_ref):
    for i, g_ref in enumerate((g0_ref, g1_ref, g2_ref, g3_ref)):
        o_ref[0, 0, i * D_MODEL:(i + 1) * D_MODEL, :] = g_ref[0].astype(BF16)
    r0 = N_BRANCH * D_MODEL
    o_ref[0, 0, r0:r0 + N_BRANCH * BRANCH_W, :] = wb_ref[0].astype(BF16)
    r0 += N_BRANCH * BRANCH_W
    for j in range(D_MODEL // WBLK):
        o_ref[0, 0, r0 + j * WBLK:r0 + (j + 1) * WBLK, :] = wo_ref[0, :, j * WBLK:(j + 1) * WBLK].astype(BF16)


def _pack_weights(w_in, w_branch, w_out):
    depth = w_in.shape[0]
    nb = D_MODEL // WBLK
    nsub = INPROJ_TN // WBLK
    w_mix_t = pl.pallas_call(
        functools.partial(_pack_mix_body, nsub=nsub),
        grid=(depth, MIX_COLS // INPROJ_TN),
        in_specs=[pl.BlockSpec((1, D_MODEL, INPROJ_TN), lambda l, j: (l, 0, j))],
        out_specs=pl.BlockSpec((1, nsub, D_MODEL, WBLK), lambda l, j: (l, j, 0, 0)),
        out_shape=jax.ShapeDtypeStruct((depth, MIX_COLS // WBLK, D_MODEL, WBLK), BF16),
        compiler_params=_cparams(2),
        name="pack_mix",
    )(w_in)

    gate_blk0 = MIX_COLS // WBLK
    gate_spec = lambda i: pl.BlockSpec((1, D_MODEL, WBLK), lambda l, c: (l, 0, gate_blk0 + i * nb + c))
    w_merge_t = pl.pallas_call(
        _pack_merge_body,
        grid=(depth, nb),
        in_specs=[
            gate_spec(0), gate_spec(1), gate_spec(2), gate_spec(3),
            pl.BlockSpec((1, N_BRANCH * BRANCH_W, WBLK), lambda l, c: (l, 0, c)),
            pl.BlockSpec((1, WBLK, D_MODEL), lambda l, c: (l, (c + nb - 1) % nb, 0)),
        ],
        out_specs=pl.BlockSpec((1, 1, MERGE_ROWS, WBLK), lambda l, c: (l, c, 0, 0)),
        out_shape=jax.ShapeDtypeStruct((depth, nb, MERGE_ROWS, WBLK), BF16),
        compiler_params=_cparams(2),
        name="pack_merge",
    )(w_in, w_in, w_in, w_in, w_branch.reshape(depth, N_BRANCH * BRANCH_W, D_MODEL), w_out)
    return w_mix_t, w_merge_t


def _bias_body(tab_ref, perm_ref, out_ref, *, n_var, tq, w, w_valid, wpad, banded):
    t = tab_ref[0] * LOG2E
    perm = perm_ref[...]
    t_hi = t.astype(BF16)
    r1 = t - t_hi.astype(F32)
    t_mid = r1.astype(BF16)
    t_lo = (r1 - t_mid.astype(F32)).astype(BF16)
    row0 = _dot(t_hi, perm) + _dot(t_mid, perm) + _dot(t_lo, perm)
    qi = lax.broadcasted_iota(jnp.int32, (tq, w), 0)
    kj = lax.broadcasted_iota(jnp.int32, (tq, w), 1)
    visible = kj < w_valid
    if banded:
        lo = (qi // CHUNK) * CHUNK
        visible = visible & (kj >= lo) & (kj < lo + ATT_BAND + CHUNK)
    for h in range(ATT_HEADS):
        full = jnp.broadcast_to(row0[h:h + 1, :], (tq, wpad))
        shifted = pltpu.roll(full, 0, 1, stride=1, stride_axis=0)[:, :w]
        for u in range(n_var):
            vis_u = visible & (kj >= ATT_BAND - u * tq) if banded else visible
            out_ref[0, u, h] = jnp.where(vis_u, shifted, NEG_BIG)


def _bias_perm(w, wpad):
    j = np.arange(wpad)
    m = np.clip(ATT_BAND - j, -MAX_REL, MAX_REL) + MAX_REL
    m = np.where(j >= w, 2 * MAX_REL, m)
    perm = np.zeros((TABLE_LANES, wpad), np.float32)
    perm[m, j] = 1.0
    return jnp.asarray(perm, BF16)


def _rel_bias(table_pad, n_var, tq, w, w_valid, banded):
    depth = table_pad.shape[0]
    wpad = -(-(w + tq) // LANES) * LANES
    return pl.pallas_call(
        functools.partial(_bias_body, n_var=n_var, tq=tq, w=w, w_valid=w_valid, wpad=wpad, banded=banded),
        grid=(depth,),
        in_specs=[
            pl.BlockSpec((1, ATT_HEADS, TABLE_LANES), lambda l: (l, 0, 0)),
            pl.BlockSpec((TABLE_LANES, wpad), lambda l: (0, 0)),
        ],
        out_specs=pl.BlockSpec((1, n_var, ATT_HEADS, tq, w), lambda l: (l, 0, 0, 0, 0)),
        out_shape=jax.ShapeDtypeStruct((depth, n_var, ATT_HEADS, tq, w), F32),
        compiler_params=_cparams(1),
        name="rel_bias",
    )(table_pad, _bias_perm(w, wpad))


def _rotary(x, cos, sin_signed):
    lane = lax.broadcasted_iota(jnp.int32, x.shape, 1)
    first_half = (lane % RET_DK) < (RET_DK // 2)
    width = x.shape[1]
    swapped = jnp.where(first_half, pltpu.roll(x, width - RET_DK // 2, 1), pltpu.roll(x, RET_DK // 2, 1))
    return x * cos + swapped * sin_signed


def _retention(h_ref, cos, sin_signed, dmat_ref, qdec, kdec, blk_ref, smask, states, br_ref):
    length = h_ref.shape[0]
    q = _rotary(h_ref[:, C_RQ:C_RQ + 256].astype(F32), cos, sin_signed)
    k = _rotary(h_ref[:, C_RK:C_RK + 256].astype(F32), cos, sin_signed) * (RET_DK ** -0.5)
    kb = k.astype(BF16)
    qd, kd = (q * qdec).astype(BF16), (k * kdec).astype(BF16)
    first = lax.broadcasted_iota(jnp.int32, (length, LANES), 1) < RET_DK
    new_states = []
    for p in range(RET_HEADS // 2):
        cols = slice(p * LANES, (p + 1) * LANES)
        sp = states[p]
        v_pair = h_ref[:, C_RV + p * 2 * RET_DV:C_RV + (p + 1) * 2 * RET_DV]
        o_inter = _dot(qd[:, cols], sp.astype(BF16))
        q2 = jnp.concatenate([jnp.where(first, q[:, cols], 0.0), jnp.where(first, 0.0, q[:, cols])],
                             axis=0).astype(BF16)
        decay2 = dmat_ref[2 * p:2 * p + 2].reshape(2 * length, length)
        inner = (_dot_nt(q2, kb[:, cols]) * decay2).astype(BF16)
        o2 = _dot(inner, v_pair)
        for half in range(2):
            hh = 2 * p + half
            vl = slice(half * RET_DV, (half + 1) * RET_DV)
            o = o2[half * length:(half + 1) * length, vl] + o_inter[:, vl]
            mu = jnp.mean(o, axis=-1, keepdims=True)
            oc = o - mu
            var = jnp.mean(oc * oc, axis=-1, keepdims=True)
            hn = oc * lax.rsqrt(var + NORM_EPS)
            gcols = slice(C_RG + hh * RET_DV, C_RG + (hh + 1) * RET_DV)
            br_ref[:, hh * RET_DV:(hh + 1) * RET_DV] = (_silu(h_ref[:, gcols].astype(F32)) * hn).astype(BF16)
        upd = _dot_tn(kd[:, cols], v_pair)
        new_states.append((sp * blk_ref[p] + upd) * smask)
    return new_states


def _conv(h_ref, prev8, convw, br_ref):
    z = h_ref[:, C_CC:C_CC + BRANCH_W].astype(F32) * h_ref[:, C_CX:C_CX + BRANCH_W].astype(F32)
    zcat = jnp.concatenate([prev8, z], axis=0)
    z1 = pltpu.roll(zcat, 1, 0)[8:]
    z2 = pltpu.roll(zcat, 2, 0)[8:]
    y = convw[0:1] * z2 + convw[1:2] * z1 + convw[2:3] * z
    cb = h_ref[:, C_CB:C_CB + BRANCH_W].astype(F32)
    cg = h_ref[:, C_CG:C_CG + BRANCH_W].astype(F32)
    br_ref[:, BRANCH_W:2 * BRANCH_W] = (_silu(cg) * (cb * y)).astype(BF16)
    return z


def _gmlp(h_ref, lng, lnb, ws_ref, bst_ref, br_ref, blk_len):
    length = h_ref.shape[0]
    mv = h_ref[:, C_MV:C_MV + BRANCH_W].astype(F32)
    mean = jnp.mean(mv, axis=-1, keepdims=True)
    cen = mv - mean
    var = jnp.mean(cen * cen, axis=-1, keepdims=True)
    vn = cen * lax.rsqrt(var + NORM_EPS) * lng + lnb
    vnb = vn.astype(BF16)
    ri = lax.broadcasted_iota(jnp.int32, (blk_len, blk_len), 0)
    ci = lax.broadcasted_iota(jnp.int32, (blk_len, blk_len), 1)
    causal = (ci // CHUNK) <= (ri // CHUNK)
    n_blocks = length // blk_len
    for g in range(GMLP_GROUPS):
        wg = jnp.where(causal, ws_ref[0, g, 0:blk_len, 0:blk_len], 0.0).astype(BF16)
        bcol = bst_ref[0, 0:blk_len, g:g + 1]
        gc = slice(g * LANES, (g + 1) * LANES)
        side = jnp.concatenate([vnb[n * blk_len:(n + 1) * blk_len, gc] for n in range(n_blocks)], axis=1)
        mix_all = _dot(wg, side) + bcol
        for n in range(n_blocks):
            rows = slice(n * blk_len, (n + 1) * blk_len)
            mix = mix_all[:, n * LANES:(n + 1) * LANES]
            mu = h_ref[rows, C_MU + g * LANES:C_MU + (g + 1) * LANES].astype(F32)
            mg = h_ref[rows, C_MG + g * LANES:C_MG + (g + 1) * LANES].astype(F32)
            br_ref[rows, 2 * BRANCH_W + g * LANES:2 * BRANCH_W + (g + 1) * LANES] = (
                _silu(mg) * (mu * mix)).astype(BF16)
    return vn


def _attention(h_ref, kh_ref, vh_ref, bias_ref, br_ref, r0, rows, c0, width):
    lane = lax.broadcasted_iota(jnp.int32, (rows, LANES), 1)
    rsl = slice(r0, r0 + rows)
    for p in range(ATT_HEADS // 2):
        cols = slice(p * LANES, (p + 1) * LANES)
        qp = h_ref[rsl, C_AQ + p * LANES:C_AQ + (p + 1) * LANES].astype(F32) * (LOG2E * ATT_DH ** -0.5)
        kp = kh_ref[c0:c0 + width, cols]
        vp = vh_ref[c0:c0 + width, cols]
        first = lane < ATT_DH
        q2 = jnp.concatenate([jnp.where(first, qp, 0.0), jnp.where(first, 0.0, qp)], axis=0).astype(BF16)
        s = _dot_nt(q2, kp) + bias_ref[0, 2 * p:2 * p + 2].reshape(2 * rows, width)
        m = jnp.max(s, axis=-1, keepdims=True)
        e = jnp.exp2(s - m)
        denom = jnp.sum(e, axis=-1, keepdims=True)
        o = _dot(e.astype(BF16), vp) / denom
        acc = jnp.where(first, o[0:rows], o[rows:2 * rows])
        ag = h_ref[rsl, C_AG + p * LANES:C_AG + (p + 1) * LANES].astype(F32)
        br_ref[rsl, 3 * BRANCH_W + p * LANES:3 * BRANCH_W + (p + 1) * LANES] = (_silu(ag) * acc).astype(BF16)


def _attention_sample(h_ref, kt, vt, ak, av, bias_ref, br_ref):
    length = h_ref.shape[0]
    n_old = kt.shape[1]
    lane_head = lax.broadcasted_iota(jnp.int32, (length, BRANCH_W), 1) // ATT_DH
    q = h_ref[:, C_AQ:C_AQ + BRANCH_W].astype(F32) * (LOG2E * ATT_DH ** -0.5)
    q_st = jnp.concatenate([jnp.where(lane_head == hh, q, 0.0) for hh in range(ATT_HEADS)], axis=0).astype(BF16)
    bias = bias_ref[0, 0].reshape(ATT_HEADS * length, bias_ref.shape[-1])
    s_old = _dot(q_st, kt) + bias[:, 0:n_old]
    s_new = _dot_nt(q_st, ak) + bias[:, n_old:n_old + length]
    m = jnp.maximum(jnp.max(s_old, axis=-1, keepdims=True), jnp.max(s_new, axis=-1, keepdims=True))
    e_old = jnp.exp2(s_old - m)
    e_new = jnp.exp2(s_new - m)
    denom = jnp.sum(e_old, axis=-1, keepdims=True) + jnp.sum(e_new, axis=-1, keepdims=True)
    o_st = (_dot_nt(e_old.astype(BF16), vt) + _dot(e_new.astype(BF16), av)) / denom
    o = None
    for hh in range(ATT_HEADS):
        part = jnp.where(lane_head == hh, o_st[hh * length:(hh + 1) * length], 0.0)
        o = part if o is None else o + part
    ag = h_ref[:, C_AG:C_AG + BRANCH_W].astype(F32)
    br_ref[:, 3 * BRANCH_W:4 * BRANCH_W] = (_silu(ag) * o).astype(BF16)


def _state_to_pairs(state_ref):
    pairs = []
    zero = jnp.zeros((RET_DK, RET_DV), F32)
    for p in range(RET_HEADS // 2):
        top = jnp.concatenate([state_ref[2 * p], zero], axis=1)
        bot = jnp.concatenate([zero, state_ref[2 * p + 1]], axis=1)
        pairs.append(jnp.concatenate([top, bot], axis=0))
    return pairs


def _pairs_to_state(pairs, out_ref):
    for p in range(RET_HEADS // 2):
        out_ref[2 * p] = pairs[p][0:RET_DK, 0:RET_DV]
        out_ref[2 * p + 1] = pairs[p][RET_DK:2 * RET_DK, RET_DV:2 * RET_DV]


def _mixer_prompt_body(h_ref, cos_ref, sin_ref, dmat_ref, qdec_ref, kdec_ref, blk_ref, smask_ref,
                       convw_ref, lng_ref, lnb_ref, ws_ref, bst_ref, *rest, tt, t_keep):
    n_blk = tt // ATT_ROWS
    bias_refs = rest[:n_blk]
    br_ref, sret_ref, sconv_ref, ko_ref, vo_ref, sp_sc, zc_sc, kh_sc, vh_sc = rest[n_blk:]
    t = pl.program_id(1)

    @pl.when(t == 0)
    def _():
        sp_sc[...] = jnp.zeros_like(sp_sc)
        zc_sc[...] = jnp.zeros_like(zc_sc)
        kh_sc[0:ATT_BAND] = jnp.zeros((ATT_BAND, BRANCH_W), BF16)
        vh_sc[0:ATT_BAND] = jnp.zeros((ATT_BAND, BRANCH_W), BF16)

    states = _retention(h_ref, cos_ref[...], sin_ref[...], dmat_ref, qdec_ref[...], kdec_ref[...], blk_ref,
                        smask_ref[...], [sp_sc[0], sp_sc[1]], br_ref)
    sp_sc[0] = states[0]
    sp_sc[1] = states[1]
    _pairs_to_state(states, sret_ref.at[0])

    z = _conv(h_ref, zc_sc[...], convw_ref[0], br_ref)
    tail = z[tt - (CONV_W - 1):tt]
    zc_sc[8 - (CONV_W - 1):8] = tail
    sconv_ref[0] = tail

    _gmlp(h_ref, lng_ref[0], lnb_ref[0], ws_ref, bst_ref, br_ref, GMLP_BLOCK)

    ak = h_ref[:, C_AK:C_AK + BRANCH_W]
    av = h_ref[:, C_AV:C_AV + BRANCH_W]
    kh_sc[ATT_BAND:ATT_BAND + tt] = ak
    vh_sc[ATT_BAND:ATT_BAND + tt] = av
    for blk, bias_ref in enumerate(bias_refs):
        _attention(h_ref, kh_sc, vh_sc, bias_ref.at[0], br_ref, blk * ATT_ROWS, ATT_ROWS, blk * ATT_ROWS,
                   ATT_BAND + ATT_ROWS)
    k_keep = kh_sc[tt:tt + ATT_BAND]
    v_keep = vh_sc[tt:tt + ATT_BAND]
    kh_sc[0:ATT_BAND] = k_keep
    vh_sc[0:ATT_BAND] = v_keep

    @pl.when(t >= t_keep)
    def _():
        ko_ref[0] = ak.astype(F32).T
        vo_ref[0] = av.astype(F32).T


def _mixer_prompt(h, tabs, conv_w, ln_g3, ln_b3, ws, bst, bias, layer, batch, seq, tt):
    nt = seq // tt
    t_keep = (seq - ATT_BAND) // tt
    full = lambda shape: pl.BlockSpec(shape, lambda b, t: (0,) * len(shape))
    per_layer = lambda shape: pl.BlockSpec((1,) + shape, lambda b, t: (layer,) + (0,) * len(shape))
    w = ATT_BAND + tt
    n_blk = tt // ATT_ROWS
    n_var = bias.shape[1]
    bias_spec = lambda blk: pl.BlockSpec((1, 1, ATT_HEADS, ATT_ROWS, ATT_BAND + ATT_ROWS),
                                         lambda b, t: (layer, jnp.minimum(n_blk * t + blk, n_var - 1), 0, 0, 0))
    return pl.pallas_call(
        functools.partial(_mixer_prompt_body, tt=tt, t_keep=t_keep),
        grid=(batch, nt),
        in_specs=[
            pl.BlockSpec((tt, MIX_COLS), lambda b, t: (b * nt + t, 0)),
            pl.BlockSpec((tt, 256), lambda b, t: (t, 0)),
            pl.BlockSpec((tt, 256), lambda b, t: (t, 0)),
            full((RET_HEADS, tt, tt)),
            full((tt, 256)), full((tt, 256)),
            full((2, 1, 256)),
            full((2 * RET_DK, 2 * RET_DV)),
            per_layer((CONV_W, BRANCH_W)),
            per_layer((1, BRANCH_W)), per_layer((1, BRANCH_W)),
            per_layer((GMLP_GROUPS, GMLP_BLOCK, GMLP_BLOCK)),
            per_layer((GMLP_BLOCK, GMLP_GROUPS)),
        ] + [bias_spec(blk) for blk in range(n_blk)],
        out_specs=[
            pl.BlockSpec((tt, D_MODEL), lambda b, t: (b * nt + t, 0)),
            pl.BlockSpec((1, RET_HEADS, RET_DK, RET_DV), lambda b, t: (b, 0, 0, 0)),
            pl.BlockSpec((1, CONV_W - 1, BRANCH_W), lambda b, t: (b, 0, 0)),
            pl.BlockSpec((1, BRANCH_W, tt), lambda b, t: (b, 0, jnp.maximum(t - t_keep, 0))),
            pl.BlockSpec((1, BRANCH_W, tt), lambda b, t: (b, 0, jnp.maximum(t - t_keep, 0))),
        ],
        out_shape=[
            jax.ShapeDtypeStruct((batch * seq, D_MODEL), BF16),
            jax.ShapeDtypeStruct((batch, RET_HEADS, RET_DK, RET_DV), F32),
            jax.ShapeDtypeStruct((batch, CONV_W - 1, BRANCH_W), F32),
            jax.ShapeDtypeStruct((batch, BRANCH_W, ATT_BAND), F32),
            jax.ShapeDtypeStruct((batch, BRANCH_W, ATT_BAND), F32),
        ],
        scratch_shapes=[
            pltpu.VMEM((2, 2 * RET_DK, 2 * RET_DV), F32),
            pltpu.VMEM((8, BRANCH_W), F32),
            pltpu.VMEM((w, BRANCH_W), BF16),
            pltpu.VMEM((w, BRANCH_W), BF16),
        ],
        compiler_params=_cparams(2),
        name="mixer_prompt",
    )(h, tabs["cos"], tabs["sin"], tabs["dmat"], tabs["qdec"], tabs["kdec"], tabs["blk"], tabs["smask"],
      conv_w, ln_g3, ln_b3, ws, bst, *([bias] * n_blk))


def _mixer_sample_body(h_ref, cos_ref, sin_ref, dmat_ref, qdec_ref, kdec_ref, blk_ref, smask_ref,
                       convw_ref, lng_ref, lnb_ref, ws_ref, bst_ref, bias_ref,
                       sret_in_ref, sconv_in_ref, ck_ref, cv_ref,
                       br_ref, sret_ref, sconv_ref, ko_ref, vo_ref, gv_ref, *, length, group):
    for g in range(group):
        hg = h_ref.at[g * length:(g + 1) * length]
        brg = br_ref.at[g * length:(g + 1) * length]

        states = _retention(hg, cos_ref[...], sin_ref[...], dmat_ref, qdec_ref[...], kdec_ref[...], blk_ref,
                            smask_ref[...], _state_to_pairs(sret_in_ref.at[0, g]), brg)
        _pairs_to_state(states, sret_ref.at[g])

        prev8 = jnp.concatenate([jnp.zeros((8 - (CONV_W - 1), BRANCH_W), F32), sconv_in_ref[0, g]], axis=0)
        z = _conv(hg, prev8, convw_ref[0], brg)
        sconv_ref[g] = z[length - (CONV_W - 1):length]

        gv_ref[g] = _gmlp(hg, lng_ref[0], lnb_ref[0], ws_ref, bst_ref, brg, length)

        ak = hg[:, C_AK:C_AK + BRANCH_W]
        av = hg[:, C_AV:C_AV + BRANCH_W]
        ko_ref[g] = ak.astype(F32)
        vo_ref[g] = av.astype(F32)
        _attention_sample(hg, ck_ref[0, g].astype(BF16), cv_ref[0, g].astype(BF16), ak, av, bias_ref, brg)


def _mixer_sample(h, tabs, conv_w, ln_g3, ln_b3, ws, bst, bias, state_ret, state_conv, cache_k, cache_v,
                  layer, streams, length, group):
    full = lambda shape: pl.BlockSpec(shape, lambda s: (0,) * len(shape))
    per_layer = lambda shape: pl.BlockSpec((1,) + shape, lambda s: (layer,) + (0,) * len(shape))
    per_stream = lambda shape: pl.BlockSpec((1, group) + shape, lambda s: (layer, s) + (0,) * len(shape))
    out_stream = lambda shape: pl.BlockSpec((group,) + shape, lambda s: (s,) + (0,) * len(shape))
    w = bias.shape[-1]
    return pl.pallas_call(
        functools.partial(_mixer_sample_body, length=length, group=group),
        grid=(streams // group,),
        in_specs=[
            pl.BlockSpec((group * length, MIX_COLS), lambda s: (s, 0)),
            full((length, 256)), full((length, 256)),
            full((RET_HEADS, length, length)),
            full((length, 256)), full((length, 256)),
            full((2, 1, 256)),
            full((2 * RET_DK, 2 * RET_DV)),
            per_layer((CONV_W, BRANCH_W)),
            per_layer((1, BRANCH_W)), per_layer((1, BRANCH_W)),
            per_layer((GMLP_GROUPS, GMLP_BLOCK, GMLP_BLOCK)),
            per_layer((GMLP_BLOCK, GMLP_GROUPS)),
            per_layer((1, ATT_HEADS, length, w)),
            per_stream((RET_HEADS, RET_DK, RET_DV)),
            per_stream((CONV_W - 1, BRANCH_W)),
            per_stream((BRANCH_W, ATT_BAND)),
            per_stream((BRANCH_W, ATT_BAND)),
        ],
        out_specs=[
            pl.BlockSpec((group * length, D_MODEL), lambda s: (s, 0)),
            out_stream((RET_HEADS, RET_DK, RET_DV)),
            out_stream((CONV_W - 1, BRANCH_W)),
            out_stream((length, BRANCH_W)),
            out_stream((length, BRANCH_W)),
            out_stream((length, BRANCH_W)),
        ],
        out_shape=[
            jax.ShapeDtypeStruct((streams * length, D_MODEL), BF16),
            jax.ShapeDtypeStruct((streams, RET_HEADS, RET_DK, RET_DV), F32),
            jax.ShapeDtypeStruct((streams, CONV_W - 1, BRANCH_W), F32),
            jax.ShapeDtypeStruct((streams, length, BRANCH_W), F32),
            jax.ShapeDtypeStruct((streams, length, BRANCH_W), F32),
            jax.ShapeDtypeStruct((streams, length, BRANCH_W), F32),
        ],
        compiler_params=_cparams(1),
        name="mixer_sample",
    )(h, tabs["cos"], tabs["sin"], tabs["dmat"], tabs["qdec"], tabs["kdec"], tabs["blk"], tabs["smask"],
      conv_w, ln_g3, ln_b3, ws, bst, bias, state_ret, state_conv, cache_k, cache_v)


def _mixer_tables(positions, blk_len):
    half = RET_DK // 2
    freqs = ROPE_BASE ** (-jnp.arange(half, dtype=F32) / half)
    ang = jnp.asarray(positions, F32)[:, None] * freqs[None, :]
    cos, sin = jnp.cos(ang), jnp.sin(ang)
    cos_t = jnp.tile(cos, (1, 2 * RET_HEADS))
    sin_t = jnp.tile(jnp.concatenate([-sin, sin], axis=1), (1, RET_HEADS))
    lg = jnp.log1p(-jnp.exp2(-5.0 - jnp.arange(RET_HEADS, dtype=F32)))
    t = jnp.arange(blk_len, dtype=F32)
    diff = t[:, None] - t[None, :]
    dmat = jnp.where(diff >= 0, jnp.exp(jnp.maximum(diff, 0.0)[None] * lg[:, None, None]), 0.0)
    qdec = jnp.repeat(jnp.exp((t[:, None] + 1.0) * lg[None, :]), RET_DK, axis=1)
    kdec = jnp.repeat(jnp.exp((blk_len - 1.0 - t)[:, None] * lg[None, :]), RET_DK, axis=1)
    blk = jnp.repeat(jnp.exp(blk_len * lg), RET_DV).reshape(RET_HEADS // 2, 1, 2 * RET_DV)
    smask = jnp.kron(jnp.eye(2, dtype=F32), jnp.ones((RET_DK, RET_DV), F32))
    return dict(cos=cos_t, sin=sin_t, dmat=dmat, qdec=qdec, kdec=kdec, blk=blk, smask=smask)


def _pick_tile(n, target):
    t = min(n, target)
    while n % t:
        t //= 2
    return t


def kernel(x_prompt, x_sample, state_ret, state_conv, cache_att_k, cache_att_v, norm_gain, w_in, w_branch, w_out,
           conv_w, gmlp_ln_gain, gmlp_ln_bias, gmlp_ws, gmlp_bs, att_rel_bias, final_norm_gain):
    batch, seq, _ = x_prompt.shape
    streams, dec_len, _ = x_sample.shape
    depth = w_in.shape[0]
    assert seq % MIXER_TT == 0 and seq >= ATT_BAND and cache_att_k.shape[2] == ATT_BAND

    tt = MIXER_TT
    n_p, n_s = batch * seq, streams * dec_len
    tm_p, tm_s = _pick_tile(n_p, 1024), _pick_tile(n_s, 512)

    w_mix_t, w_merge_t = _pack_weights(w_in, w_branch, w_out)
    gain3 = norm_gain.reshape(depth, 1, D_MODEL)
    ln_g3 = gmlp_ln_gain.reshape(depth, 1, BRANCH_W)
    ln_b3 = gmlp_ln_bias.reshape(depth, 1, BRANCH_W)
    bst = jnp.swapaxes(gmlp_bs, 1, 2)
    table_pad = jnp.pad(att_rel_bias, ((0, 0), (0, 0), (0, TABLE_LANES - att_rel_bias.shape[-1])))
    to_t = lambda c: jnp.transpose(c, (0, 1, 3, 4, 2)).reshape(depth, streams, BRANCH_W, ATT_BAND)
    cache_k, cache_v = to_t(cache_att_k), to_t(cache_att_v)
    gain2 = final_norm_gain.reshape(1, D_MODEL)

    tabs_p = _mixer_tables(np.arange(seq), tt)
    tabs_s = _mixer_tables(PAST_LEN + np.arange(dec_len), dec_len)

    hp = x_prompt.reshape(n_p, D_MODEL)
    hs = x_sample.reshape(n_s, D_MODEL)
    outs = [[] for _ in range(9)]
    w_p = ATT_BAND + ATT_ROWS
    bias_p = _rel_bias(table_pad, ATT_BAND // ATT_ROWS + 1, ATT_ROWS, w_p, w_p, True)
    w_s = -(-(ATT_BAND + dec_len) // LANES) * LANES
    bias_s = _rel_bias(table_pad, 1, dec_len, w_s, ATT_BAND + dec_len, False)
    for l in range(depth):
        h_mix, xn = _inproj(hp, gain3, w_mix_t, l, tm_p, INPROJ_TN)
        br, p_ret, p_conv, p_k, p_v = _mixer_prompt(h_mix, tabs_p, conv_w, ln_g3, ln_b3, gmlp_ws, bst, bias_p,
                                                    l, batch, seq, tt)
        hp = _merge(xn, br, hp, w_merge_t, gain2, l, tm_p, l == depth - 1)

        h_mix, xn = _inproj(hs, gain3, w_mix_t, l, tm_s, INPROJ_TN)
        br, s_ret, s_conv, s_k, s_v, s_gv = _mixer_sample(h_mix, tabs_s, conv_w, ln_g3, ln_b3, gmlp_ws, bst, bias_s,
                                                          state_ret, state_conv, cache_k, cache_v,
                                                          l, streams, dec_len, _pick_tile(streams, SAMPLE_GROUP))
        hs = _merge(xn, br, hs, w_merge_t, gain2, l, tm_s, l == depth - 1)

        for dst, val in zip(outs, (p_ret, p_conv, p_k, p_v, s_ret, s_conv, s_k, s_v, s_gv)):
            dst.append(val)

    y_prompt = hp.reshape(batch, seq, D_MODEL)
    y_sample = hs.reshape(streams, dec_len, D_MODEL)
    p_ret, p_conv, p_k, p_v, s_ret, s_conv, s_k, s_v, s_gv = [jnp.stack(o) for o in outs]
    kv_shape = lambda a: a.reshape(a.shape[:3] + (ATT_HEADS, ATT_DH))
    kv_from_t = lambda a: jnp.transpose(a.reshape(a.shape[:2] + (ATT_HEADS, ATT_DH, ATT_BAND)), (0, 1, 4, 2, 3))
    return (y_prompt, y_sample, p_ret, p_conv, kv_from_t(p_k), kv_from_t(p_v),
            s_ret, s_conv, kv_shape(s_k), kv_shape(s_v), s_gv)
```

```python
import functools

import numpy as np
import jax
import jax.numpy as jnp
from jax import lax
from jax.experimental import pallas as pl
from jax.experimental.pallas import tpu as pltpu

F32 = jnp.float32
BF16 = jnp.bfloat16

D_MODEL = 2048
BRANCH_W = 512
N_BRANCH = 4
CHUNK = 64
RET_HEADS = 4
RET_DK = 64
RET_DV = 128
ROPE_BASE = 10000.0
CONV_W = 3
GMLP_BLOCK = 128
GMLP_GROUPS = 4
ATT_HEADS = 8
ATT_DH = 64
ATT_BAND = 512
MAX_REL = 128
NORM_EPS = 1e-6
PAST_LEN = 2048
NEG_BIG = -1e30

MIX_COLS = 2 * RET_HEADS * RET_DK + 13 * BRANCH_W
IN_COLS = MIX_COLS + N_BRANCH * D_MODEL
C_RQ, C_RK, C_RV, C_RG = 0, 256, 512, 1024
C_CB, C_CC, C_CX, C_CG = 1536, 2048, 2560, 3072
C_MU, C_MV, C_MG = 3584, 4096, 4608
C_AQ, C_AK, C_AV, C_AG = 5120, 5632, 6144, 6656

LANES = 128
WBLK = 256
INPROJ_TN = 7 * WBLK
ATT_ROWS = 256
MIXER_TT = 256
MERGE_ROW_CHUNK = 256
NORM_ROW_CHUNK = 256
SAMPLE_GROUP = 4
LOG2E = 1.4426950408889634
VMEM_LIMIT_BYTES = 56 * 1024 * 1024
TABLE_LANES = 384


def _cparams(n_axes):
    return pltpu.CompilerParams(dimension_semantics=("arbitrary",) * n_axes,
                                vmem_limit_bytes=VMEM_LIMIT_BYTES)


def _dot(a, b):
    return jnp.dot(a, b, preferred_element_type=F32)


def _dot_nt(a, b):
    return lax.dot_general(a, b, (((1,), (1,)), ((), ())), preferred_element_type=F32)


def _dot_tn(a, b):
    return lax.dot_general(a, b, (((0,), (0,)), ((), ())), preferred_element_type=F32)


def _silu(x):
    return x * jax.nn.sigmoid(x)


def _inproj_body(x_ref, g_ref, w_ref, h_ref, xn_ref, *, nsub):
    j = pl.program_id(1)
    tm = x_ref.shape[0]
    rc = min(tm, NORM_ROW_CHUNK)

    @pl.when(j == 0)
    def _():
        for r0 in range(0, tm, rc):
            x = x_ref[r0:r0 + rc, :]
            inv = lax.rsqrt(jnp.mean(x * x, axis=-1, keepdims=True) + NORM_EPS)
            xn = (x * inv * g_ref[0]).astype(BF16)
            xn_ref[r0:r0 + rc, :] = xn
            for c in range(nsub):
                h_ref[r0:r0 + rc, c * WBLK:(c + 1) * WBLK] = _dot(xn, w_ref[0, c]).astype(h_ref.dtype)

    @pl.when(j > 0)
    def _():
        xn = xn_ref[...]
        for c in range(nsub):
            h_ref[:, c * WBLK:(c + 1) * WBLK] = _dot(xn, w_ref[0, c]).astype(h_ref.dtype)


def _inproj(x2d, gain3, w_mix_t, layer, tm, tn):
    n = x2d.shape[0]
    nsub = tn // WBLK
    return pl.pallas_call(
        functools.partial(_inproj_body, nsub=nsub),
        grid=(n // tm, MIX_COLS // tn),
        in_specs=[
            pl.BlockSpec((tm, D_MODEL), lambda i, j: (i, 0)),
            pl.BlockSpec((1, 1, D_MODEL), lambda i, j: (layer, 0, 0)),
            pl.BlockSpec((1, nsub, D_MODEL, WBLK), lambda i, j: (layer, j, 0, 0)),
        ],
        out_specs=[
            pl.BlockSpec((tm, tn), lambda i, j: (i, j)),
            pl.BlockSpec((tm, D_MODEL), lambda i, j: (i, 0)),
        ],
        out_shape=[
            jax.ShapeDtypeStruct((n, MIX_COLS), BF16),
            jax.ShapeDtypeStruct((n, D_MODEL), BF16),
        ],
        compiler_params=_cparams(2),
        name="inproj",
    )(x2d, gain3, w_mix_t)


MERGE_ROWS = N_BRANCH * D_MODEL + N_BRANCH * BRANCH_W + D_MODEL
MERGE_WO_ROW0 = N_BRANCH * D_MODEL + N_BRANCH * BRANCH_W


def _add_out_projection(hm_sc, wo_ref, out_ref):
    hm = hm_sc[...]
    for j in range(D_MODEL // WBLK):
        out_ref[:, j * WBLK:(j + 1) * WBLK] += _dot(hm, wo_ref[j * WBLK:(j + 1) * WBLK, :])


def _merge_body(xn_ref, br_ref, x_ref, w_ref, wlast_ref, *rest, final_norm):
    fg_ref, out_ref, hm_sc = rest if final_norm else (None,) + rest
    s = pl.program_id(1)
    nb = D_MODEL // WBLK
    tm = out_ref.shape[0]

    @pl.when(s == 0)
    def _():
        out_ref[...] = jnp.zeros_like(out_ref)

    @pl.when(s > 0)
    def _():
        _add_out_projection(hm_sc, w_ref.at[0, 0, MERGE_WO_ROW0:MERGE_ROWS], out_ref)

    rc = min(tm, MERGE_ROW_CHUNK)
    for r in range(tm // rc):
        rows = slice(r * rc, (r + 1) * rc)
        xn = xn_ref[rows, :]
        acc = None
        for i in range(N_BRANCH):
            logits = _dot(xn, w_ref[0, 0, i * D_MODEL:(i + 1) * D_MODEL, :])
            wb0 = N_BRANCH * D_MODEL + i * BRANCH_W
            proj = _dot(br_ref[rows, i * BRANCH_W:(i + 1) * BRANCH_W], w_ref[0, 0, wb0:wb0 + BRANCH_W, :])
            term = jax.nn.sigmoid(logits) * proj
            acc = term if acc is None else acc + term
        hm_sc[rows, :] = acc.astype(BF16)

    @pl.when(s == nb - 1)
    def _():
        _add_out_projection(hm_sc, wlast_ref.at[0, 0], out_ref)

    for c in range(nb):
        @pl.when(s == c)
        def _():
            out_ref[:, c * WBLK:(c + 1) * WBLK] += x_ref[...]

    if final_norm:
        @pl.when(s == nb - 1)
        def _():
            y = out_ref[...]
            r = lax.rsqrt(jnp.mean(y * y, axis=-1, keepdims=True) + NORM_EPS)
            out_ref[...] = y * r * fg_ref[...]


def _merge(xn, br, x2d, w_merge_t, final_gain2, layer, tm, final_norm):
    n = x2d.shape[0]
    nb = D_MODEL // WBLK
    wo_part = MERGE_WO_ROW0 // D_MODEL
    in_specs = [
        pl.BlockSpec((tm, D_MODEL), lambda m, s: (m, 0)),
        pl.BlockSpec((tm, D_MODEL), lambda m, s: (m, 0)),
        pl.BlockSpec((tm, WBLK), lambda m, s: (m, s)),
        pl.BlockSpec((1, 1, MERGE_ROWS, WBLK), lambda m, s: (layer, s, 0, 0)),
        pl.BlockSpec((1, 1, D_MODEL, WBLK), lambda m, s: (layer, 0, wo_part, 0)),
    ]
    operands = [xn, br, x2d, w_merge_t, w_merge_t]
    if final_norm:
        in_specs.append(pl.BlockSpec((1, D_MODEL), lambda m, s: (0, 0)))
        operands.append(final_gain2)
    return pl.pallas_call(
        functools.partial(_merge_body, final_norm=final_norm),
        grid=(n // tm, nb),
        in_specs=in_specs,
        out_specs=pl.BlockSpec((tm, D_MODEL), lambda m, s: (m, 0)),
        out_shape=jax.ShapeDtypeStruct((n, D_MODEL), F32),
        scratch_shapes=[pltpu.VMEM((tm, WBLK), BF16)],
        compiler_params=_cparams(2),
        name="merge",
    )(*operands)


def _pack_mix_body(w_ref, o_ref, *, nsub):
    for c in range(nsub):
        o_ref[0, c] = w_ref[0, :, c * WBLK:(c + 1) * WBLK].astype(BF16)


def _pack_merge_body(g0_ref, g1_ref, g2_ref, g3_ref, wb_ref, wo_ref, o_ref):
    for i, g_ref in enumerate((g0_ref, g1_ref, g2_ref, g3_ref)):
        o_ref[0, 0, i * D_MODEL:(i + 1) * D_MODEL, :] = g_ref[0].astype(BF16)
    r0 = N_BRANCH * D_MODEL
    o_ref[0, 0, r0:r0 + N_BRANCH * BRANCH_W, :] = wb_ref[0].astype(BF16)
    r0 += N_BRANCH * BRANCH_W
    for j in range(D_MODEL // WBLK):
        o_ref[0, 0, r0 + j * WBLK:r0 + (j + 1) * WBLK, :] = wo_ref[0, :, j * WBLK:(j + 1) * WBLK].astype(BF16)


def _pack_weights(w_in, w_branch, w_out):
    depth = w_in.shape[0]
    nb = D_MODEL // WBLK
    nsub = INPROJ_TN // WBLK
    w_mix_t = pl.pallas_call(
        functools.partial(_pack_mix_body, nsub=nsub),
        grid=(depth, MIX_COLS // INPROJ_TN),
        in_specs=[pl.BlockSpec((1, D_MODEL, INPROJ_TN), lambda l, j: (l, 0, j))],
        out_specs=pl.BlockSpec((1, nsub, D_MODEL, WBLK), lambda l, j: (l, j, 0, 0)),
        out_shape=jax.ShapeDtypeStruct((depth, MIX_COLS // WBLK, D_MODEL, WBLK), BF16),
        compiler_params=_cparams(2),
        name="pack_mix",
    )(w_in)

    gate_blk0 = MIX_COLS // WBLK
    gate_spec = lambda i: pl.BlockSpec((1, D_MODEL, WBLK), lambda l, c: (l, 0, gate_blk0 + i * nb + c))
    w_merge_t = pl.pallas_call(
        _pack_merge_body,
        grid=(depth, nb),
        in_specs=[
            gate_spec(0), gate_spec(1), gate_spec(2), gate_spec(3),
            pl.BlockSpec((1, N_BRANCH * BRANCH_W, WBLK), lambda l, c: (l, 0, c)),
            pl.BlockSpec((1, WBLK, D_MODEL), lambda l, c: (l, (c + nb - 1) % nb, 0)),
        ],
        out_specs=pl.BlockSpec((1, 1, MERGE_ROWS, WBLK), lambda l, c: (l, c, 0, 0)),
        out_shape=jax.ShapeDtypeStruct((depth, nb, MERGE_ROWS, WBLK), BF16),
        compiler_params=_cparams(2),
        name="pack_merge",
    )(w_in, w_in, w_in, w_in, w_branch.reshape(depth, N_BRANCH * BRANCH_W, D_MODEL), w_out)
    return w_mix_t, w_merge_t


def _bias_body(tab_ref, perm_ref, out_ref, *, n_var, tq, w, w_valid, wpad, banded):
    t = tab_ref[0] * LOG2E
    perm = perm_ref[...]
    t_hi = t.astype(BF16)
    r1 = t - t_hi.astype(F32)
    t_mid = r1.astype(BF16)
    t_lo = (r1 - t_mid.astype(F32)).astype(BF16)
    row0 = _dot(t_hi, perm) + _dot(t_mid, perm) + _dot(t_lo, perm)
    qi = lax.broadcasted_iota(jnp.int32, (tq, w), 0)
    kj = lax.broadcasted_iota(jnp.int32, (tq, w), 1)
    visible = kj < w_valid
    if banded:
        lo = (qi // CHUNK) * CHUNK
        visible = visible & (kj >= lo) & (kj < lo + ATT_BAND + CHUNK)
    for h in range(ATT_HEADS):
        full = jnp.broadcast_to(row0[h:h + 1, :], (tq, wpad))
        shifted = pltpu.roll(full, 0, 1, stride=1, stride_axis=0)[:, :w]
        for u in range(n_var):
            vis_u = visible & (kj >= ATT_BAND - u * tq) if banded else visible
            out_ref[0, u, h] = jnp.where(vis_u, shifted, NEG_BIG)


def _bias_perm(w, wpad):
    j = np.arange(wpad)
    m = np.clip(ATT_BAND - j, -MAX_REL, MAX_REL) + MAX_REL
    m = np.where(j >= w, 2 * MAX_REL, m)
    perm = np.zeros((TABLE_LANES, wpad), np.float32)
    perm[m, j] = 1.0
    return jnp.asarray(perm, BF16)


def _rel_bias(table_pad, n_var, tq, w, w_valid, banded):
    depth = table_pad.shape[0]
    wpad = -(-(w + tq) // LANES) * LANES
    return pl.pallas_call(
        functools.partial(_bias_body, n_var=n_var, tq=tq, w=w, w_valid=w_valid, wpad=wpad, banded=banded),
        grid=(depth,),
        in_specs=[
            pl.BlockSpec((1, ATT_HEADS, TABLE_LANES), lambda l: (l, 0, 0)),
            pl.BlockSpec((TABLE_LANES, wpad), lambda l: (0, 0)),
        ],
        out_specs=pl.BlockSpec((1, n_var, ATT_HEADS, tq, w), lambda l: (l, 0, 0, 0, 0)),
        out_shape=jax.ShapeDtypeStruct((depth, n_var, ATT_HEADS, tq, w), F32),
        compiler_params=_cparams(1),
        name="rel_bias",
    )(table_pad, _bias_perm(w, wpad))


def _rotary(x, cos, sin_signed):
    lane = lax.broadcasted_iota(jnp.int32, x.shape, 1)
    first_half = (lane % RET_DK) < (RET_DK // 2)
    width = x.shape[1]
    swapped = jnp.where(first_half, pltpu.roll(x, width - RET_DK // 2, 1), pltpu.roll(x, RET_DK // 2, 1))
    return x * cos + swapped * sin_signed


def _retention(h_ref, cos, sin_signed, dmat_ref, qdec, kdec, blk_ref, smask, states, br_ref):
    length = h_ref.shape[0]
    q = _rotary(h_ref[:, C_RQ:C_RQ + 256].astype(F32), cos, sin_signed)
    k = _rotary(h_ref[:, C_RK:C_RK + 256].astype(F32), cos, sin_signed) * (RET_DK ** -0.5)
    kb = k.astype(BF16)
    qd, kd = (q * qdec).astype(BF16), (k * kdec).astype(BF16)
    first = lax.broadcasted_iota(jnp.int32, (length, LANES), 1) < RET_DK
    new_states = []
    for p in range(RET_HEADS // 2):
        cols = slice(p * LANES, (p + 1) * LANES)
        sp = states[p]
        v_pair = h_ref[:, C_RV + p * 2 * RET_DV:C_RV + (p + 1) * 2 * RET_DV]
        o_inter = _dot(qd[:, cols], sp.astype(BF16))
        q2 = jnp.concatenate([jnp.where(first, q[:, cols], 0.0), jnp.where(first, 0.0, q[:, cols])],
                             axis=0).astype(BF16)
        decay2 = dmat_ref[2 * p:2 * p + 2].reshape(2 * length, length)
        inner = (_dot_nt(q2, kb[:, cols]) * decay2).astype(BF16)
        o2 = _dot(inner, v_pair)
        for half in range(2):
            hh = 2 * p + half
            vl = slice(half * RET_DV, (half + 1) * RET_DV)
            o = o2[half * length:(half + 1) * length, vl] + o_inter[:, vl]
            mu = jnp.mean(o, axis=-1, keepdims=True)
            oc = o - mu
            var = jnp.mean(oc * oc, axis=-1, keepdims=True)
            hn = oc * lax.rsqrt(var + NORM_EPS)
            gcols = slice(C_RG + hh * RET_DV, C_RG + (hh + 1) * RET_DV)
            br_ref[:, hh * RET_DV:(hh + 1) * RET_DV] = (_silu(h_ref[:, gcols].astype(F32)) * hn).astype(BF16)
        upd = _dot_tn(kd[:, cols], v_pair)
        new_states.append((sp * blk_ref[p] + upd) * smask)
    return new_states


def _conv(h_ref, prev8, convw, br_ref):
    z = h_ref[:, C_CC:C_CC + BRANCH_W].astype(F32) * h_ref[:, C_CX:C_CX + BRANCH_W].astype(F32)
    zcat = jnp.concatenate([prev8, z], axis=0)
    z1 = pltpu.roll(zcat, 1, 0)[8:]
    z2 = pltpu.roll(zcat, 2, 0)[8:]
    y = convw[0:1] * z2 + convw[1:2] * z1 + convw[2:3] * z
    cb = h_ref[:, C_CB:C_CB + BRANCH_W].astype(F32)
    cg = h_ref[:, C_CG:C_CG + BRANCH_W].astype(F32)
    br_ref[:, BRANCH_W:2 * BRANCH_W] = (_silu(cg) * (cb * y)).astype(BF16)
    return z


def _gmlp(h_ref, lng, lnb, ws_ref, bst_ref, br_ref, blk_len):
    length = h_ref.shape[0]
    mv = h_ref[:, C_MV:C_MV + BRANCH_W].astype(F32)
    mean = jnp.mean(mv, axis=-1, keepdims=True)
    cen = mv - mean
    var = jnp.mean(cen * cen, axis=-1, keepdims=True)
    vn = cen * lax.rsqrt(var + NORM_EPS) * lng + lnb
    vnb = vn.astype(BF16)
    ri = lax.broadcasted_iota(jnp.int32, (blk_len, blk_len), 0)
    ci = lax.broadcasted_iota(jnp.int32, (blk_len, blk_len), 1)
    causal = (ci // CHUNK) <= (ri // CHUNK)
    n_blocks = length // blk_len
    for g in range(GMLP_GROUPS):
        wg = jnp.where(causal, ws_ref[0, g, 0:blk_len, 0:blk_len], 0.0).astype(BF16)
        bcol = bst_ref[0, 0:blk_len, g:g + 1]
        gc = slice(g * LANES, (g + 1) * LANES)
        side = jnp.concatenate([vnb[n * blk_len:(n + 1) * blk_len, gc] for n in range(n_blocks)], axis=1)
        mix_all = _dot(wg, side) + bcol
        for n in range(n_blocks):
            rows = slice(n * blk_len, (n + 1) * blk_len)
            mix = mix_all[:, n * LANES:(n + 1) * LANES]
            mu = h_ref[rows, C_MU + g * LANES:C_MU + (g + 1) * LANES].astype(F32)
            mg = h_ref[rows, C_MG + g * LANES:C_MG + (g + 1) * LANES].astype(F32)
            br_ref[rows, 2 * BRANCH_W + g * LANES:2 * BRANCH_W + (g + 1) * LANES] = (
                _silu(mg) * (mu * mix)).astype(BF16)
    return vn


def _attention(h_ref, kh_ref, vh_ref, bias_ref, br_ref, r0, rows, c0, width):
    lane = lax.broadcasted_iota(jnp.int32, (rows, LANES), 1)
    rsl = slice(r0, r0 + rows)
    for p in range(ATT_HEADS // 2):
        cols = slice(p * LANES, (p + 1) * LANES)
        qp = h_ref[rsl, C_AQ + p * LANES:C_AQ + (p + 1) * LANES].astype(F32) * (LOG2E * ATT_DH ** -0.5)
        kp = kh_ref[c0:c0 + width, cols]
        vp = vh_ref[c0:c0 + width, cols]
        first = lane < ATT_DH
        q2 = jnp.concatenate([jnp.where(first, qp, 0.0), jnp.where(first, 0.0, qp)], axis=0).astype(BF16)
        s = _dot_nt(q2, kp) + bias_ref[0, 2 * p:2 * p + 2].reshape(2 * rows, width)
        m = jnp.max(s, axis=-1, keepdims=True)
        e = jnp.exp2(s - m)
        denom = jnp.sum(e, axis=-1, keepdims=True)
        o = _dot(e.astype(BF16), vp) / denom
        acc = jnp.where(first, o[0:rows], o[rows:2 * rows])
        ag = h_ref[rsl, C_AG + p * LANES:C_AG + (p + 1) * LANES].astype(F32)
        br_ref[rsl, 3 * BRANCH_W + p * LANES:3 * BRANCH_W + (p + 1) * LANES] = (_silu(ag) * acc).astype(BF16)


def _attention_sample(h_ref, kt, vt, ak, av, bias_ref, br_ref):
    length = h_ref.shape[0]
    n_old = kt.shape[1]
    lane_head = lax.broadcasted_iota(jnp.int32, (length, BRANCH_W), 1) // ATT_DH
    q = h_ref[:, C_AQ:C_AQ + BRANCH_W].astype(F32) * (LOG2E * ATT_DH ** -0.5)
    q_st = jnp.concatenate([jnp.where(lane_head == hh, q, 0.0) for hh in range(ATT_HEADS)], axis=0).astype(BF16)
    bias = bias_ref[0, 0].reshape(ATT_HEADS * length, bias_ref.shape[-1])
    s_old = _dot(q_st, kt) + bias[:, 0:n_old]
    s_new = _dot_nt(q_st, ak) + bias[:, n_old:n_old + length]
    m = jnp.maximum(jnp.max(s_old, axis=-1, keepdims=True), jnp.max(s_new, axis=-1, keepdims=True))
    e_old = jnp.exp2(s_old - m)
    e_new = jnp.exp2(s_new - m)
    denom = jnp.sum(e_old, axis=-1, keepdims=True) + jnp.sum(e_new, axis=-1, keepdims=True)
    o_st = (_dot_nt(e_old.astype(BF16), vt) + _dot(e_new.astype(BF16), av)) / denom
    o = None
    for hh in range(ATT_HEADS):
        part = jnp.where(lane_head == hh, o_st[hh * length:(hh + 1) * length], 0.0)
        o = part if o is None else o + part
    ag = h_ref[:, C_AG:C_AG + BRANCH_W].astype(F32)
    br_ref[:, 3 * BRANCH_W:4 * BRANCH_W] = (_silu(ag) * o).astype(BF16)


def _state_to_pairs(state_ref):
    pairs = []
    zero = jnp.zeros((RET_DK, RET_DV), F32)
    for p in range(RET_HEADS // 2):
        top = jnp.concatenate([state_ref[2 * p], zero], axis=1)
        bot = jnp.concatenate([zero, state_ref[2 * p + 1]], axis=1)
        pairs.append(jnp.concatenate([top, bot], axis=0))
    return pairs


def _pairs_to_state(pairs, out_ref):
    for p in range(RET_HEADS // 2):
        out_ref[2 * p] = pairs[p][0:RET_DK, 0:RET_DV]
        out_ref[2 * p + 1] = pairs[p][RET_DK:2 * RET_DK, RET_DV:2 * RET_DV]


def _mixer_prompt_body(h_ref, cos_ref, sin_ref, dmat_ref, qdec_ref, kdec_ref, blk_ref, smask_ref,
                       convw_ref, lng_ref, lnb_ref, ws_ref, bst_ref, *rest, tt, t_keep):
    n_blk = tt // ATT_ROWS
    bias_refs = rest[:n_blk]
    br_ref, sret_ref, sconv_ref, ko_ref, vo_ref, sp_sc, zc_sc, kh_sc, vh_sc = rest[n_blk:]
    t = pl.program_id(1)

    @pl.when(t == 0)
    def _():
        sp_sc[...] = jnp.zeros_like(sp_sc)
        zc_sc[...] = jnp.zeros_like(zc_sc)
        kh_sc[0:ATT_BAND] = jnp.zeros((ATT_BAND, BRANCH_W), BF16)
        vh_sc[0:ATT_BAND] = jnp.zeros((ATT_BAND, BRANCH_W), BF16)

    states = _retention(h_ref, cos_ref[...], sin_ref[...], dmat_ref, qdec_ref[...], kdec_ref[...], blk_ref,
                        smask_ref[...], [sp_sc[0], sp_sc[1]], br_ref)
    sp_sc[0] = states[0]
    sp_sc[1] = states[1]
    _pairs_to_state(states, sret_ref.at[0])

    z = _conv(h_ref, zc_sc[...], convw_ref[0], br_ref)
    tail = z[tt - (CONV_W - 1):tt]
    zc_sc[8 - (CONV_W - 1):8] = tail
    sconv_ref[0] = tail

    _gmlp(h_ref, lng_ref[0], lnb_ref[0], ws_ref, bst_ref, br_ref, GMLP_BLOCK)

    ak = h_ref[:, C_AK:C_AK + BRANCH_W]
    av = h_ref[:, C_AV:C_AV + BRANCH_W]
    kh_sc[ATT_BAND:ATT_BAND + tt] = ak
    vh_sc[ATT_BAND:ATT_BAND + tt] = av
    for blk, bias_ref in enumerate(bias_refs):
        _attention(h_ref, kh_sc, vh_sc, bias_ref.at[0], br_ref, blk * ATT_ROWS, ATT_ROWS, blk * ATT_ROWS,
                   ATT_BAND + ATT_ROWS)
    k_keep = kh_sc[tt:tt + ATT_BAND]
    v_keep = vh_sc[tt:tt + ATT_BAND]
    kh_sc[0:ATT_BAND] = k_keep
    vh_sc[0:ATT_BAND] = v_keep

    @pl.when(t >= t_keep)
    def _():
        ko_ref[0] = ak.astype(F32).T
        vo_ref[0] = av.astype(F32).T


def _mixer_prompt(h, tabs, conv_w, ln_g3, ln_b3, ws, bst, bias, layer, batch, seq, tt):
    nt = seq // tt
    t_keep = (seq - ATT_BAND) // tt
    full = lambda shape: pl.BlockSpec(shape, lambda b, t: (0,) * len(shape))
    per_layer = lambda shape: pl.BlockSpec((1,) + shape, lambda b, t: (layer,) + (0,) * len(shape))
    w = ATT_BAND + tt
    n_blk = tt // ATT_ROWS
    n_var = bias.shape[1]
    bias_spec = lambda blk: pl.BlockSpec((1, 1, ATT_HEADS, ATT_ROWS, ATT_BAND + ATT_ROWS),
                                         lambda b, t: (layer, jnp.minimum(n_blk * t + blk, n_var - 1), 0, 0, 0))
    return pl.pallas_call(
        functools.partial(_mixer_prompt_body, tt=tt, t_keep=t_keep),
        grid=(batch, nt),
        in_specs=[
            pl.BlockSpec((tt, MIX_COLS), lambda b, t: (b * nt + t, 0)),
            pl.BlockSpec((tt, 256), lambda b, t: (t, 0)),
            pl.BlockSpec((tt, 256), lambda b, t: (t, 0)),
            full((RET_HEADS, tt, tt)),
            full((tt, 256)), full((tt, 256)),
            full((2, 1, 256)),
            full((2 * RET_DK, 2 * RET_DV)),
            per_layer((CONV_W, BRANCH_W)),
            per_layer((1, BRANCH_W)), per_layer((1, BRANCH_W)),
            per_layer((GMLP_GROUPS, GMLP_BLOCK, GMLP_BLOCK)),
            per_layer((GMLP_BLOCK, GMLP_GROUPS)),
        ] + [bias_spec(blk) for blk in range(n_blk)],
        out_specs=[
            pl.BlockSpec((tt, D_MODEL), lambda b, t: (b * nt + t, 0)),
            pl.BlockSpec((1, RET_HEADS, RET_DK, RET_DV), lambda b, t: (b, 0, 0, 0)),
            pl.BlockSpec((1, CONV_W - 1, BRANCH_W), lambda b, t: (b, 0, 0)),
            pl.BlockSpec((1, BRANCH_W, tt), lambda b, t: (b, 0, jnp.maximum(t - t_keep, 0))),
            pl.BlockSpec((1, BRANCH_W, tt), lambda b, t: (b, 0, jnp.maximum(t - t_keep, 0))),
        ],
        out_shape=[
            jax.ShapeDtypeStruct((batch * seq, D_MODEL), BF16),
            jax.ShapeDtypeStruct((batch, RET_HEADS, RET_DK, RET_DV), F32),
            jax.ShapeDtypeStruct((batch, CONV_W - 1, BRANCH_W), F32),
            jax.ShapeDtypeStruct((batch, BRANCH_W, ATT_BAND), F32),
            jax.ShapeDtypeStruct((batch, BRANCH_W, ATT_BAND), F32),
        ],
        scratch_shapes=[
            pltpu.VMEM((2, 2 * RET_DK, 2 * RET_DV), F32),
            pltpu.VMEM((8, BRANCH_W), F32),
            pltpu.VMEM((w, BRANCH_W), BF16),
            pltpu.VMEM((w, BRANCH_W), BF16),
        ],
        compiler_params=_cparams(2),
        name="mixer_prompt",
    )(h, tabs["cos"], tabs["sin"], tabs["dmat"], tabs["qdec"], tabs["kdec"], tabs["blk"], tabs["smask"],
      conv_w, ln_g3, ln_b3, ws, bst, *([bias] * n_blk))


def _mixer_sample_body(h_ref, cos_ref, sin_ref, dmat_ref, qdec_ref, kdec_ref, blk_ref, smask_ref,
                       convw_ref, lng_ref, lnb_ref, ws_ref, bst_ref, bias_ref,
                       sret_in_ref, sconv_in_ref, ck_ref, cv_ref,
                       br_ref, sret_ref, sconv_ref, ko_ref, vo_ref, gv_ref, *, length, group):
    vn = _gmlp(h_ref, lng_ref[0], lnb_ref[0], ws_ref, bst_ref, br_ref, length)
    for g in range(group):
        gv_ref[g] = vn[g * length:(g + 1) * length]

    for g in range(group):
        hg = h_ref.at[g * length:(g + 1) * length]
        brg = br_ref.at[g * length:(g + 1) * length]

        states = _retention(hg, cos_ref[...], sin_ref[...], dmat_ref, qdec_ref[...], kdec_ref[...], blk_ref,
                            smask_ref[...], _state_to_pairs(sret_in_ref.at[0, g]), brg)
        _pairs_to_state(states, sret_ref.at[g])

        prev8 = jnp.concatenate([jnp.zeros((8 - (CONV_W - 1), BRANCH_W), F32), sconv_in_ref[0, g]], axis=0)
        z = _conv(hg, prev8, convw_ref[0], brg)
        sconv_ref[g] = z[length - (CONV_W - 1):length]

        ak = hg[:, C_AK:C_AK + BRANCH_W]
        av = hg[:, C_AV:C_AV + BRANCH_W]
        ko_ref[g] = ak.astype(F32)
        vo_ref[g] = av.astype(F32)
        _attention_sample(hg, ck_ref[0, g].astype(BF16), cv_ref[0, g].astype(BF16), ak, av, bias_ref, brg)


def _mixer_sample(h, tabs, conv_w, ln_g3, ln_b3, ws, bst, bias, state_ret, state_conv, cache_k, cache_v,
                  layer, streams, length, group):
    full = lambda shape: pl.BlockSpec(shape, lambda s: (0,) * len(shape))
    per_layer = lambda shape: pl.BlockSpec((1,) + shape, lambda s: (layer,) + (0,) * len(shape))
    per_stream = lambda shape: pl.BlockSpec((1, group) + shape, lambda s: (layer, s) + (0,) * len(shape))
    out_stream = lambda shape: pl.BlockSpec((group,) + shape, lambda s: (s,) + (0,) * len(shape))
    w = bias.shape[-1]
    return pl.pallas_call(
        functools.partial(_mixer_sample_body, length=length, group=group),
        grid=(streams // group,),
        in_specs=[
            pl.BlockSpec((group * length, MIX_COLS), lambda s: (s, 0)),
            full((length, 256)), full((length, 256)),
            full((RET_HEADS, length, length)),
            full((length, 256)), full((length, 256)),
            full((2, 1, 256)),
            full((2 * RET_DK, 2 * RET_DV)),
            per_layer((CONV_W, BRANCH_W)),
            per_layer((1, BRANCH_W)), per_layer((1, BRANCH_W)),
            per_layer((GMLP_GROUPS, GMLP_BLOCK, GMLP_BLOCK)),
            per_layer((GMLP_BLOCK, GMLP_GROUPS)),
            per_layer((1, ATT_HEADS, length, w)),
            per_stream((RET_HEADS, RET_DK, RET_DV)),
            per_stream((CONV_W - 1, BRANCH_W)),
            per_stream((BRANCH_W, ATT_BAND)),
            per_stream((BRANCH_W, ATT_BAND)),
        ],
        out_specs=[
            pl.BlockSpec((group * length, D_MODEL), lambda s: (s, 0)),
            out_stream((RET_HEADS, RET_DK, RET_DV)),
            out_stream((CONV_W - 1, BRANCH_W)),
            out_stream((length, BRANCH_W)),
            out_stream((length, BRANCH_W)),
            out_stream((length, BRANCH_W)),
        ],
        out_shape=[
            jax.ShapeDtypeStruct((streams * length, D_MODEL), BF16),
            jax.ShapeDtypeStruct((streams, RET_HEADS, RET_DK, RET_DV), F32),
            jax.ShapeDtypeStruct((streams, CONV_W - 1, BRANCH_W), F32),
            jax.ShapeDtypeStruct((streams, length, BRANCH_W), F32),
            jax.ShapeDtypeStruct((streams, length, BRANCH_W), F32),
            jax.ShapeDtypeStruct((streams, length, BRANCH_W), F32),
        ],
        compiler_params=_cparams(1),
        name="mixer_sample",
    )(h, tabs["cos"], tabs["sin"], tabs["dmat"], tabs["qdec"], tabs["kdec"], tabs["blk"], tabs["smask"],
      conv_w, ln_g3, ln_b3, ws, bst, bias, state_ret, state_conv, cache_k, cache_v)


def _mixer_tables(positions, blk_len):
    half = RET_DK // 2
    freqs = ROPE_BASE ** (-jnp.arange(half, dtype=F32) / half)
    ang = jnp.asarray(positions, F32)[:, None] * freqs[None, :]
    cos, sin = jnp.cos(ang), jnp.sin(ang)
    cos_t = jnp.tile(cos, (1, 2 * RET_HEADS))
    sin_t = jnp.tile(jnp.concatenate([-sin, sin], axis=1), (1, RET_HEADS))
    lg = jnp.log1p(-jnp.exp2(-5.0 - jnp.arange(RET_HEADS, dtype=F32)))
    t = jnp.arange(blk_len, dtype=F32)
    diff = t[:, None] - t[None, :]
    dmat = jnp.where(diff >= 0, jnp.exp(jnp.maximum(diff, 0.0)[None] * lg[:, None, None]), 0.0)
    qdec = jnp.repeat(jnp.exp((t[:, None] + 1.0) * lg[None, :]), RET_DK, axis=1)
    kdec = jnp.repeat(jnp.exp((blk_len - 1.0 - t)[:, None] * lg[None, :]), RET_DK, axis=1)
    blk = jnp.repeat(jnp.exp(blk_len * lg), RET_DV).reshape(RET_HEADS // 2, 1, 2 * RET_DV)
    smask = jnp.kron(jnp.eye(2, dtype=F32), jnp.ones((RET_DK, RET_DV), F32))
    return dict(cos=cos_t, sin=sin_t, dmat=dmat, qdec=qdec, kdec=kdec, blk=blk, smask=smask)


def _pick_tile(n, target):
    t = min(n, target)
    while n % t:
        t //= 2
    return t


def kernel(x_prompt, x_sample, state_ret, state_conv, cache_att_k, cache_att_v, norm_gain, w_in, w_branch, w_out,
           conv_w, gmlp_ln_gain, gmlp_ln_bias, gmlp_ws, gmlp_bs, att_rel_bias, final_norm_gain):
    batch, seq, _ = x_prompt.shape
    streams, dec_len, _ = x_sample.shape
    depth = w_in.shape[0]
    assert seq % MIXER_TT == 0 and seq >= ATT_BAND and cache_att_k.shape[2] == ATT_BAND

    tt = MIXER_TT
    n_p, n_s = batch * seq, streams * dec_len
    tm_p, tm_s = _pick_tile(n_p, 1024), _pick_tile(n_s, 512)

    w_mix_t, w_merge_t = _pack_weights(w_in, w_branch, w_out)
    gain3 = norm_gain.reshape(depth, 1, D_MODEL)
    ln_g3 = gmlp_ln_gain.reshape(depth, 1, BRANCH_W)
    ln_b3 = gmlp_ln_bias.reshape(depth, 1, BRANCH_W)
    bst = jnp.swapaxes(gmlp_bs, 1, 2)
    table_pad = jnp.pad(att_rel_bias, ((0, 0), (0, 0), (0, TABLE_LANES - att_rel_bias.shape[-1])))
    to_t = lambda c: jnp.transpose(c, (0, 1, 3, 4, 2)).reshape(depth, streams, BRANCH_W, ATT_BAND)
    cache_k, cache_v = to_t(cache_att_k), to_t(cache_att_v)
    gain2 = final_norm_gain.reshape(1, D_MODEL)

    tabs_p = _mixer_tables(np.arange(seq), tt)
    tabs_s = _mixer_tables(PAST_LEN + np.arange(dec_len), dec_len)

    hp = x_prompt.reshape(n_p, D_MODEL)
    hs = x_sample.reshape(n_s, D_MODEL)
    outs = [[] for _ in range(9)]
    w_p = ATT_BAND + ATT_ROWS
    bias_p = _rel_bias(table_pad, ATT_BAND // ATT_ROWS + 1, ATT_ROWS, w_p, w_p, True)
    w_s = -(-(ATT_BAND + dec_len) // LANES) * LANES
    bias_s = _rel_bias(table_pad, 1, dec_len, w_s, ATT_BAND + dec_len, False)
    for l in range(depth):
        h_mix, xn = _inproj(hp, gain3, w_mix_t, l, tm_p, INPROJ_TN)
        br, p_ret, p_conv, p_k, p_v = _mixer_prompt(h_mix, tabs_p, conv_w, ln_g3, ln_b3, gmlp_ws, bst, bias_p,
                                                    l, batch, seq, tt)
        hp = _merge(xn, br, hp, w_merge_t, gain2, l, tm_p, l == depth - 1)

        h_mix, xn = _inproj(hs, gain3, w_mix_t, l, tm_s, INPROJ_TN)
        br, s_ret, s_conv, s_k, s_v, s_gv = _mixer_sample(h_mix, tabs_s, conv_w, ln_g3, ln_b3, gmlp_ws, bst, bias_s,
                                                          state_ret, state_conv, cache_k, cache_v,
                                                          l, streams, dec_len, _pick_tile(streams, SAMPLE_GROUP))
        hs = _merge(xn, br, hs, w_merge_t, gain2, l, tm_s, l == depth - 1)

        for dst, val in zip(outs, (p_ret, p_conv, p_k, p_v, s_ret, s_conv, s_k, s_v, s_gv)):
            dst.append(val)

    y_prompt = hp.reshape(batch, seq, D_MODEL)
    y_sample = hs.reshape(streams, dec_len, D_MODEL)
    p_ret, p_conv, p_k, p_v, s_ret, s_conv, s_k, s_v, s_gv = [jnp.stack(o) for o in outs]
    kv_shape = lambda a: a.reshape(a.shape[:3] + (ATT_HEADS, ATT_DH))
    kv_from_t = lambda a: jnp.transpose(a.reshape(a.shape[:2] + (ATT_HEADS, ATT_DH, ATT_BAND)), (0, 1, 4, 2, 3))
    return (y_prompt, y_sample, p_ret, p_conv, kv_from_t(p_k), kv_from_t(p_v),
            s_ret, s_conv, kv_shape(s_k), kv_shape(s_v), s_gv)
```

```python
import functools

import numpy as np
import jax
import jax.numpy as jnp
from jax import lax
from jax.experimental import pallas as pl
from jax.experimental.pallas import tpu as pltpu

F32 = jnp.float32
BF16 = jnp.bfloat16

D_MODEL = 2048
BRANCH_W = 512
N_BRANCH = 4
CHUNK = 64
RET_HEADS = 4
RET_DK = 64
RET_DV = 128
ROPE_BASE = 10000.0
CONV_W = 3
GMLP_BLOCK = 128
GMLP_GROUPS = 4
ATT_HEADS = 8
ATT_DH = 64
ATT_BAND = 512
MAX_REL = 128
NORM_EPS = 1e-6
PAST_LEN = 2048
NEG_BIG = -1e30

MIX_COLS = 2 * RET_HEADS * RET_DK + 13 * BRANCH_W
IN_COLS = MIX_COLS + N_BRANCH * D_MODEL
C_RQ, C_RK, C_RV, C_RG = 0, 256, 512, 1024
C_CB, C_CC, C_CX, C_CG = 1536, 2048, 2560, 3072
C_MU, C_MV, C_MG = 3584, 4096, 4608
C_AQ, C_AK, C_AV, C_AG = 5120, 5632, 6144, 6656

LANES = 128
WBLK = 256
INPROJ_TN = 7 * WBLK
ATT_ROWS = 256
MIXER_TT = 256
MERGE_ROW_CHUNK = 256
NORM_ROW_CHUNK = 256
SAMPLE_GROUP = 4
LOG2E = 1.4426950408889634
VMEM_LIMIT_BYTES = 56 * 1024 * 1024
TABLE_LANES = 384


def _cparams(n_axes):
    return pltpu.CompilerParams(dimension_semantics=("arbitrary",) * n_axes,
                                vmem_limit_bytes=VMEM_LIMIT_BYTES)


def _dot(a, b):
    return jnp.dot(a, b, preferred_element_type=F32)


def _dot_nt(a, b):
    return lax.dot_general(a, b, (((1,), (1,)), ((), ())), preferred_element_type=F32)


def _dot_tn(a, b):
    return lax.dot_general(a, b, (((0,), (0,)), ((), ())), preferred_element_type=F32)


def _silu(x):
    return x * jax.nn.sigmoid(x)


def _inproj_body(x_ref, g_ref, w_ref, h_ref, xn_ref, *, nsub):
    j = pl.program_id(1)
    tm = x_ref.shape[0]
    rc = min(tm, NORM_ROW_CHUNK)

    @pl.when(j == 0)
    def _():
        for r0 in range(0, tm, rc):
            x = x_ref[r0:r0 + rc, :]
            inv = lax.rsqrt(jnp.mean(x * x, axis=-1, keepdims=True) + NORM_EPS)
            xn = (x * inv * g_ref[0]).astype(BF16)
            xn_ref[r0:r0 + rc, :] = xn
            for c in range(nsub):
                h_ref[r0:r0 + rc, c * WBLK:(c + 1) * WBLK] = _dot(xn, w_ref[0, c]).astype(h_ref.dtype)

    @pl.when(j > 0)
    def _():
        xn = xn_ref[...]
        for c in range(nsub):
            h_ref[:, c * WBLK:(c + 1) * WBLK] = _dot(xn, w_ref[0, c]).astype(h_ref.dtype)


def _inproj(x2d, gain3, w_mix_t, layer, tm, tn):
    n = x2d.shape[0]
    nsub = tn // WBLK
    return pl.pallas_call(
        functools.partial(_inproj_body, nsub=nsub),
        grid=(n // tm, MIX_COLS // tn),
        in_specs=[
            pl.BlockSpec((tm, D_MODEL), lambda i, j: (i, 0)),
            pl.BlockSpec((1, 1, D_MODEL), lambda i, j: (layer, 0, 0)),
            pl.BlockSpec((1, nsub, D_MODEL, WBLK), lambda i, j: (layer, j, 0, 0)),
        ],
        out_specs=[
            pl.BlockSpec((tm, tn), lambda i, j: (i, j)),
            pl.BlockSpec((tm, D_MODEL), lambda i, j: (i, 0)),
        ],
        out_shape=[
            jax.ShapeDtypeStruct((n, MIX_COLS), BF16),
            jax.ShapeDtypeStruct((n, D_MODEL), BF16),
        ],
        compiler_params=_cparams(2),
        name="inproj",
    )(x2d, gain3, w_mix_t)


MERGE_ROWS = N_BRANCH * D_MODEL + N_BRANCH * BRANCH_W + D_MODEL
MERGE_WO_ROW0 = N_BRANCH * D_MODEL + N_BRANCH * BRANCH_W


def _add_out_projection(hm_sc, wo_ref, out_ref, first):
    hm = hm_sc[...]
    for j in range(D_MODEL // WBLK):
        upd = _dot(hm, wo_ref[j * WBLK:(j + 1) * WBLK, :])
        if first and j > 0:
            out_ref[:, j * WBLK:(j + 1) * WBLK] = upd
        else:
            out_ref[:, j * WBLK:(j + 1) * WBLK] += upd


def _merge_body(xn_ref, br_ref, x_ref, w_ref, *rest, final_norm):
    fg_ref, out_ref, hm_sc, wlast_sc = rest if final_norm else (None,) + rest
    s = pl.program_id(1)
    nb = D_MODEL // WBLK
    tm = out_ref.shape[0]
    wo_ref = w_ref.at[0, 0, MERGE_WO_ROW0:MERGE_ROWS]

    @pl.when(s == 0)
    def _():
        out_ref[:, 0:WBLK] = x_ref[...]
        wlast_sc[...] = wo_ref[...]

    @pl.when(s == 1)
    def _():
        _add_out_projection(hm_sc, wo_ref, out_ref, True)

    @pl.when(s > 1)
    def _():
        _add_out_projection(hm_sc, wo_ref, out_ref, False)

    rc = min(tm, MERGE_ROW_CHUNK)
    for r in range(tm // rc):
        rows = slice(r * rc, (r + 1) * rc)
        xn = xn_ref[rows, :]
        acc = None
        for i in range(N_BRANCH):
            logits = _dot(xn, w_ref[0, 0, i * D_MODEL:(i + 1) * D_MODEL, :])
            wb0 = N_BRANCH * D_MODEL + i * BRANCH_W
            proj = _dot(br_ref[rows, i * BRANCH_W:(i + 1) * BRANCH_W], w_ref[0, 0, wb0:wb0 + BRANCH_W, :])
            term = jax.nn.sigmoid(logits) * proj
            acc = term if acc is None else acc + term
        hm_sc[rows, :] = acc.astype(BF16)

    @pl.when(s == nb - 1)
    def _():
        _add_out_projection(hm_sc, wlast_sc, out_ref, False)

    for c in range(1, nb):
        @pl.when(s == c)
        def _():
            out_ref[:, c * WBLK:(c + 1) * WBLK] += x_ref[...]

    if final_norm:
        @pl.when(s == nb - 1)
        def _():
            y = out_ref[...]
            r = lax.rsqrt(jnp.mean(y * y, axis=-1, keepdims=True) + NORM_EPS)
            out_ref[...] = y * r * fg_ref[...]


def _merge(xn, br, x2d, w_merge_t, final_gain2, layer, tm, final_norm):
    n = x2d.shape[0]
    nb = D_MODEL // WBLK
    in_specs = [
        pl.BlockSpec((tm, D_MODEL), lambda m, s: (m, 0)),
        pl.BlockSpec((tm, D_MODEL), lambda m, s: (m, 0)),
        pl.BlockSpec((tm, WBLK), lambda m, s: (m, s)),
        pl.BlockSpec((1, 1, MERGE_ROWS, WBLK), lambda m, s: (layer, s, 0, 0)),
    ]
    operands = [xn, br, x2d, w_merge_t]
    if final_norm:
        in_specs.append(pl.BlockSpec((1, D_MODEL), lambda m, s: (0, 0)))
        operands.append(final_gain2)
    return pl.pallas_call(
        functools.partial(_merge_body, final_norm=final_norm),
        grid=(n // tm, nb),
        in_specs=in_specs,
        out_specs=pl.BlockSpec((tm, D_MODEL), lambda m, s: (m, 0)),
        out_shape=jax.ShapeDtypeStruct((n, D_MODEL), F32),
        scratch_shapes=[pltpu.VMEM((tm, WBLK), BF16), pltpu.VMEM((D_MODEL, WBLK), BF16)],
        compiler_params=_cparams(2),
        name="merge",
    )(*operands)


def _pack_mix_body(w_ref, o_ref, *, nsub):
    for c in range(nsub):
        o_ref[0, c] = w_ref[0, :, c * WBLK:(c + 1) * WBLK].astype(BF16)


def _pack_merge_body(g0_ref, g1_ref, g2_ref, g3_ref, wb_ref, wo_ref, o_ref):
    for i, g_ref in enumerate((g0_ref, g1_ref, g2_ref, g3_ref)):
        o_ref[0, 0, i * D_MODEL:(i + 1) * D_MODEL, :] = g_ref[0].astype(BF16)
    r0 = N_BRANCH * D_MODEL
    o_ref[0, 0, r0:r0 + N_BRANCH * BRANCH_W, :] = wb_ref[0].astype(BF16)
    r0 += N_BRANCH * BRANCH_W
    for j in range(D_MODEL // WBLK):
        o_ref[0, 0, r0 + j * WBLK:r0 + (j + 1) * WBLK, :] = wo_ref[0, :, j * WBLK:(j + 1) * WBLK].astype(BF16)


def _pack_weights(w_in, w_branch, w_out):
    depth = w_in.shape[0]
    nb = D_MODEL // WBLK
    nsub = INPROJ_TN // WBLK
    w_mix_t = pl.pallas_call(
        functools.partial(_pack_mix_body, nsub=nsub),
        grid=(depth, MIX_COLS // INPROJ_TN),
        in_specs=[pl.BlockSpec((1, D_MODEL, INPROJ_TN), lambda l, j: (l, 0, j))],
        out_specs=pl.BlockSpec((1, nsub, D_MODEL, WBLK), lambda l, j: (l, j, 0, 0)),
        out_shape=jax.ShapeDtypeStruct((depth, MIX_COLS // WBLK, D_MODEL, WBLK), BF16),
        compiler_params=_cparams(2),
        name="pack_mix",
    )(w_in)

    gate_blk0 = MIX_COLS // WBLK
    gate_spec = lambda i: pl.BlockSpec((1, D_MODEL, WBLK), lambda l, c: (l, 0, gate_blk0 + i * nb + c))
    w_merge_t = pl.pallas_call(
        _pack_merge_body,
        grid=(depth, nb),
        in_specs=[
            gate_spec(0), gate_spec(1), gate_spec(2), gate_spec(3),
            pl.BlockSpec((1, N_BRANCH * BRANCH_W, WBLK), lambda l, c: (l, 0, c)),
            pl.BlockSpec((1, WBLK, D_MODEL), lambda l, c: (l, (c + nb - 1) % nb, 0)),
        ],
        out_specs=pl.BlockSpec((1, 1, MERGE_ROWS, WBLK), lambda l, c: (l, c, 0, 0)),
        out_shape=jax.ShapeDtypeStruct((depth, nb, MERGE_ROWS, WBLK), BF16),
        compiler_params=_cparams(2),
        name="pack_merge",
    )(w_in, w_in, w_in, w_in, w_branch.reshape(depth, N_BRANCH * BRANCH_W, D_MODEL), w_out)
    return w_mix_t, w_merge_t


def _bias_body(tab_ref, perm_ref, out_ref, *, n_var, tq, w, w_valid, wpad, banded):
    t = tab_ref[0] * LOG2E
    perm = perm_ref[...]
    t_hi = t.astype(BF16)
    r1 = t - t_hi.astype(F32)
    t_mid = r1.astype(BF16)
    t_lo = (r1 - t_mid.astype(F32)).astype(BF16)
    row0 = _dot(t_hi, perm) + _dot(t_mid, perm) + _dot(t_lo, perm)
    qi = lax.broadcasted_iota(jnp.int32, (tq, w), 0)
    kj = lax.broadcasted_iota(jnp.int32, (tq, w), 1)
    visible = kj < w_valid
    if banded:
        lo = (qi // CHUNK) * CHUNK
        visible = visible & (kj >= lo) & (kj < lo + ATT_BAND + CHUNK)
    for h in range(ATT_HEADS):
        full = jnp.broadcast_to(row0[h:h + 1, :], (tq, wpad))
        shifted = pltpu.roll(full, 0, 1, stride=1, stride_axis=0)[:, :w]
        for u in range(n_var):
            vis_u = visible & (kj >= ATT_BAND - u * tq) if banded else visible
            out_ref[0, u, h] = jnp.where(vis_u, shifted, NEG_BIG)


def _bias_perm(w, wpad):
    j = np.arange(wpad)
    m = np.clip(ATT_BAND - j, -MAX_REL, MAX_REL) + MAX_REL
    m = np.where(j >= w, 2 * MAX_REL, m)
    perm = np.zeros((TABLE_LANES, wpad), np.float32)
    perm[m, j] = 1.0
    return jnp.asarray(perm, BF16)


def _rel_bias(table_pad, n_var, tq, w, w_valid, banded):
    depth = table_pad.shape[0]
    wpad = -(-(w + tq) // LANES) * LANES
    return pl.pallas_call(
        functools.partial(_bias_body, n_var=n_var, tq=tq, w=w, w_valid=w_valid, wpad=wpad, banded=banded),
        grid=(depth,),
        in_specs=[
            pl.BlockSpec((1, ATT_HEADS, TABLE_LANES), lambda l: (l, 0, 0)),
            pl.BlockSpec((TABLE_LANES, wpad), lambda l: (0, 0)),
        ],
        out_specs=pl.BlockSpec((1, n_var, ATT_HEADS, tq, w), lambda l: (l, 0, 0, 0, 0)),
        out_shape=jax.ShapeDtypeStruct((depth, n_var, ATT_HEADS, tq, w), F32),
        compiler_params=_cparams(1),
        name="rel_bias",
    )(table_pad, _bias_perm(w, wpad))


def _rotary(x, cos, sin_signed):
    lane = lax.broadcasted_iota(jnp.int32, x.shape, 1)
    first_half = (lane % RET_DK) < (RET_DK // 2)
    width = x.shape[1]
    swapped = jnp.where(first_half, pltpu.roll(x, width - RET_DK // 2, 1), pltpu.roll(x, RET_DK // 2, 1))
    return x * cos + swapped * sin_signed


def _retention(h_ref, cos, sin_signed, dmat_ref, qdec, kdec, blk_ref, smask, states, br_ref):
    length = h_ref.shape[0]
    q = _rotary(h_ref[:, C_RQ:C_RQ + 256].astype(F32), cos, sin_signed)
    k = _rotary(h_ref[:, C_RK:C_RK + 256].astype(F32), cos, sin_signed) * (RET_DK ** -0.5)
    kb = k.astype(BF16)
    qd, kd = (q * qdec).astype(BF16), (k * kdec).astype(BF16)
    first = lax.broadcasted_iota(jnp.int32, (length, LANES), 1) < RET_DK
    new_states = []
    for p in range(RET_HEADS // 2):
        cols = slice(p * LANES, (p + 1) * LANES)
        sp = states[p]
        v_pair = h_ref[:, C_RV + p * 2 * RET_DV:C_RV + (p + 1) * 2 * RET_DV]
        o_inter = _dot(qd[:, cols], sp.astype(BF16))
        q2 = jnp.concatenate([jnp.where(first, q[:, cols], 0.0), jnp.where(first, 0.0, q[:, cols])],
                             axis=0).astype(BF16)
        decay2 = dmat_ref[2 * p:2 * p + 2].reshape(2 * length, length)
        inner = (_dot_nt(q2, kb[:, cols]) * decay2).astype(BF16)
        o2 = _dot(inner, v_pair)
        for half in range(2):
            hh = 2 * p + half
            vl = slice(half * RET_DV, (half + 1) * RET_DV)
            o = o2[half * length:(half + 1) * length, vl] + o_inter[:, vl]
            mu = jnp.mean(o, axis=-1, keepdims=True)
            oc = o - mu
            var = jnp.mean(oc * oc, axis=-1, keepdims=True)
            hn = oc * lax.rsqrt(var + NORM_EPS)
            gcols = slice(C_RG + hh * RET_DV, C_RG + (hh + 1) * RET_DV)
            br_ref[:, hh * RET_DV:(hh + 1) * RET_DV] = (_silu(h_ref[:, gcols].astype(F32)) * hn).astype(BF16)
        upd = _dot_tn(kd[:, cols], v_pair)
        new_states.append((sp * blk_ref[p] + upd) * smask)
    return new_states


def _conv(h_ref, prev8, convw, br_ref):
    z = h_ref[:, C_CC:C_CC + BRANCH_W].astype(F32) * h_ref[:, C_CX:C_CX + BRANCH_W].astype(F32)
    zcat = jnp.concatenate([prev8, z], axis=0)
    z1 = pltpu.roll(zcat, 1, 0)[8:]
    z2 = pltpu.roll(zcat, 2, 0)[8:]
    y = convw[0:1] * z2 + convw[1:2] * z1 + convw[2:3] * z
    cb = h_ref[:, C_CB:C_CB + BRANCH_W].astype(F32)
    cg = h_ref[:, C_CG:C_CG + BRANCH_W].astype(F32)
    br_ref[:, BRANCH_W:2 * BRANCH_W] = (_silu(cg) * (cb * y)).astype(BF16)
    return z


def _gmlp(h_ref, lng, lnb, ws_ref, bst_ref, br_ref, blk_len):
    length = h_ref.shape[0]
    mv = h_ref[:, C_MV:C_MV + BRANCH_W].astype(F32)
    mean = jnp.mean(mv, axis=-1, keepdims=True)
    cen = mv - mean
    var = jnp.mean(cen * cen, axis=-1, keepdims=True)
    vn = cen * lax.rsqrt(var + NORM_EPS) * lng + lnb
    vnb = vn.astype(BF16)
    ri = lax.broadcasted_iota(jnp.int32, (blk_len, blk_len), 0)
    ci = lax.broadcasted_iota(jnp.int32, (blk_len, blk_len), 1)
    causal = (ci // CHUNK) <= (ri // CHUNK)
    n_blocks = length // blk_len
    for g in range(GMLP_GROUPS):
        wg = jnp.where(causal, ws_ref[0, g, 0:blk_len, 0:blk_len], 0.0).astype(BF16)
        bcol = bst_ref[0, 0:blk_len, g:g + 1]
        gc = slice(g * LANES, (g + 1) * LANES)
        side = jnp.concatenate([vnb[n * blk_len:(n + 1) * blk_len, gc] for n in range(n_blocks)], axis=1)
        mix_all = _dot(wg, side) + bcol
        for n in range(n_blocks):
            rows = slice(n * blk_len, (n + 1) * blk_len)
            mix = mix_all[:, n * LANES:(n + 1) * LANES]
            mu = h_ref[rows, C_MU + g * LANES:C_MU + (g + 1) * LANES].astype(F32)
            mg = h_ref[rows, C_MG + g * LANES:C_MG + (g + 1) * LANES].astype(F32)
            br_ref[rows, 2 * BRANCH_W + g * LANES:2 * BRANCH_W + (g + 1) * LANES] = (
                _silu(mg) * (mu * mix)).astype(BF16)
    return vn


def _attention(h_ref, kh_ref, vh_ref, bias_ref, br_ref, r0, rows, c0, width):
    lane = lax.broadcasted_iota(jnp.int32, (rows, LANES), 1)
    rsl = slice(r0, r0 + rows)
    for p in range(ATT_HEADS // 2):
        cols = slice(p * LANES, (p + 1) * LANES)
        qp = h_ref[rsl, C_AQ + p * LANES:C_AQ + (p + 1) * LANES].astype(F32) * (LOG2E * ATT_DH ** -0.5)
        kp = kh_ref[c0:c0 + width, cols]
        vp = vh_ref[c0:c0 + width, cols]
        first = lane < ATT_DH
        q2 = jnp.concatenate([jnp.where(first, qp, 0.0), jnp.where(first, 0.0, qp)], axis=0).astype(BF16)
        s = _dot_nt(q2, kp) + bias_ref[0, 2 * p:2 * p + 2].reshape(2 * rows, width)
        m = jnp.max(s, axis=-1, keepdims=True)
        e = jnp.exp2(s - m)
        denom = jnp.sum(e, axis=-1, keepdims=True)
        o = _dot(e.astype(BF16), vp) / denom
        acc = jnp.where(first, o[0:rows], o[rows:2 * rows])
        ag = h_ref[rsl, C_AG + p * LANES:C_AG + (p + 1) * LANES].astype(F32)
        br_ref[rsl, 3 * BRANCH_W + p * LANES:3 * BRANCH_W + (p + 1) * LANES] = (_silu(ag) * acc).astype(BF16)


def _attention_sample(h_ref, kt, vt, ak, av, bias_ref, br_ref):
    length = h_ref.shape[0]
    n_old = kt.shape[1]
    lane_head = lax.broadcasted_iota(jnp.int32, (length, BRANCH_W), 1) // ATT_DH
    q = h_ref[:, C_AQ:C_AQ + BRANCH_W].astype(F32) * (LOG2E * ATT_DH ** -0.5)
    q_st = jnp.concatenate([jnp.where(lane_head == hh, q, 0.0) for hh in range(ATT_HEADS)], axis=0).astype(BF16)
    bias = bias_ref[0, 0].reshape(ATT_HEADS * length, bias_ref.shape[-1])
    s_old = _dot(q_st, kt) + bias[:, 0:n_old]
    s_new = _dot_nt(q_st, ak) + bias[:, n_old:n_old + length]
    m = jnp.maximum(jnp.max(s_old, axis=-1, keepdims=True), jnp.max(s_new, axis=-1, keepdims=True))
    e_old = jnp.exp2(s_old - m)
    e_new = jnp.exp2(s_new - m)
    denom = jnp.sum(e_old, axis=-1, keepdims=True) + jnp.sum(e_new, axis=-1, keepdims=True)
    o_st = (_dot_nt(e_old.astype(BF16), vt) + _dot(e_new.astype(BF16), av)) / denom
    o = None
    for hh in range(ATT_HEADS):
        part = jnp.where(lane_head == hh, o_st[hh * length:(hh + 1) * length], 0.0)
        o = part if o is None else o + part
    ag = h_ref[:, C_AG:C_AG + BRANCH_W].astype(F32)
    br_ref[:, 3 * BRANCH_W:4 * BRANCH_W] = (_silu(ag) * o).astype(BF16)


def _state_to_pairs(state_ref):
    pairs = []
    zero = jnp.zeros((RET_DK, RET_DV), F32)
    for p in range(RET_HEADS // 2):
        top = jnp.concatenate([state_ref[2 * p], zero], axis=1)
        bot = jnp.concatenate([zero, state_ref[2 * p + 1]], axis=1)
        pairs.append(jnp.concatenate([top, bot], axis=0))
    return pairs


def _pairs_to_state(pairs, out_ref):
    for p in range(RET_HEADS // 2):
        out_ref[2 * p] = pairs[p][0:RET_DK, 0:RET_DV]
        out_ref[2 * p + 1] = pairs[p][RET_DK:2 * RET_DK, RET_DV:2 * RET_DV]


def _mixer_prompt_body(h_ref, cos_ref, sin_ref, dmat_ref, qdec_ref, kdec_ref, blk_ref, smask_ref,
                       convw_ref, lng_ref, lnb_ref, ws_ref, bst_ref, *rest, tt, t_keep):
    n_blk = tt // ATT_ROWS
    bias_refs = rest[:n_blk]
    br_ref, sret_ref, sconv_ref, ko_ref, vo_ref, sp_sc, zc_sc, kh_sc, vh_sc = rest[n_blk:]
    t = pl.program_id(1)

    @pl.when(t == 0)
    def _():
        sp_sc[...] = jnp.zeros_like(sp_sc)
        zc_sc[...] = jnp.zeros_like(zc_sc)
        kh_sc[0:ATT_BAND] = jnp.zeros((ATT_BAND, BRANCH_W), BF16)
        vh_sc[0:ATT_BAND] = jnp.zeros((ATT_BAND, BRANCH_W), BF16)

    states = _retention(h_ref, cos_ref[...], sin_ref[...], dmat_ref, qdec_ref[...], kdec_ref[...], blk_ref,
                        smask_ref[...], [sp_sc[0], sp_sc[1]], br_ref)
    sp_sc[0] = states[0]
    sp_sc[1] = states[1]
    _pairs_to_state(states, sret_ref.at[0])

    z = _conv(h_ref, zc_sc[...], convw_ref[0], br_ref)
    tail = z[tt - (CONV_W - 1):tt]
    zc_sc[8 - (CONV_W - 1):8] = tail
    sconv_ref[0] = tail

    _gmlp(h_ref, lng_ref[0], lnb_ref[0], ws_ref, bst_ref, br_ref, GMLP_BLOCK)

    ak = h_ref[:, C_AK:C_AK + BRANCH_W]
    av = h_ref[:, C_AV:C_AV + BRANCH_W]
    kh_sc[ATT_BAND:ATT_BAND + tt] = ak
    vh_sc[ATT_BAND:ATT_BAND + tt] = av
    for blk, bias_ref in enumerate(bias_refs):
        _attention(h_ref, kh_sc, vh_sc, bias_ref.at[0], br_ref, blk * ATT_ROWS, ATT_ROWS, blk * ATT_ROWS,
                   ATT_BAND + ATT_ROWS)
    k_keep = kh_sc[tt:tt + ATT_BAND]
    v_keep = vh_sc[tt:tt + ATT_BAND]
    kh_sc[0:ATT_BAND] = k_keep
    vh_sc[0:ATT_BAND] = v_keep

    @pl.when(t >= t_keep)
    def _():
        ko_ref[0] = ak.astype(F32).T
        vo_ref[0] = av.astype(F32).T


def _mixer_prompt(h, tabs, conv_w, ln_g3, ln_b3, ws, bst, bias, layer, batch, seq, tt):
    nt = seq // tt
    t_keep = (seq - ATT_BAND) // tt
    full = lambda shape: pl.BlockSpec(shape, lambda b, t: (0,) * len(shape))
    per_layer = lambda shape: pl.BlockSpec((1,) + shape, lambda b, t: (layer,) + (0,) * len(shape))
    w = ATT_BAND + tt
    n_blk = tt // ATT_ROWS
    n_var = bias.shape[1]
    bias_spec = lambda blk: pl.BlockSpec((1, 1, ATT_HEADS, ATT_ROWS, ATT_BAND + ATT_ROWS),
                                         lambda b, t: (layer, jnp.minimum(n_blk * t + blk, n_var - 1), 0, 0, 0))
    return pl.pallas_call(
        functools.partial(_mixer_prompt_body, tt=tt, t_keep=t_keep),
        grid=(batch, nt),
        in_specs=[
            pl.BlockSpec((tt, MIX_COLS), lambda b, t: (b * nt + t, 0)),
            pl.BlockSpec((tt, 256), lambda b, t: (t, 0)),
            pl.BlockSpec((tt, 256), lambda b, t: (t, 0)),
            full((RET_HEADS, tt, tt)),
            full((tt, 256)), full((tt, 256)),
            full((2, 1, 256)),
            full((2 * RET_DK, 2 * RET_DV)),
            per_layer((CONV_W, BRANCH_W)),
            per_layer((1, BRANCH_W)), per_layer((1, BRANCH_W)),
            per_layer((GMLP_GROUPS, GMLP_BLOCK, GMLP_BLOCK)),
            per_layer((GMLP_BLOCK, GMLP_GROUPS)),
        ] + [bias_spec(blk) for blk in range(n_blk)],
        out_specs=[
            pl.BlockSpec((tt, D_MODEL), lambda b, t: (b * nt + t, 0)),
            pl.BlockSpec((1, RET_HEADS, RET_DK, RET_DV), lambda b, t: (b, 0, 0, 0)),
            pl.BlockSpec((1, CONV_W - 1, BRANCH_W), lambda b, t: (b, 0, 0)),
            pl.BlockSpec((1, BRANCH_W, tt), lambda b, t: (b, 0, jnp.maximum(t - t_keep, 0))),
            pl.BlockSpec((1, BRANCH_W, tt), lambda b, t: (b, 0, jnp.maximum(t - t_keep, 0))),
        ],
        out_shape=[
            jax.ShapeDtypeStruct((batch * seq, D_MODEL), BF16),
            jax.ShapeDtypeStruct((batch, RET_HEADS, RET_DK, RET_DV), F32),
            jax.ShapeDtypeStruct((batch, CONV_W - 1, BRANCH_W), F32),
            jax.ShapeDtypeStruct((batch, BRANCH_W, ATT_BAND), F32),
            jax.ShapeDtypeStruct((batch, BRANCH_W, ATT_BAND), F32),
        ],
        scratch_shapes=[
            pltpu.VMEM((2, 2 * RET_DK, 2 * RET_DV), F32),
            pltpu.VMEM((8, BRANCH_W), F32),
            pltpu.VMEM((w, BRANCH_W), BF16),
            pltpu.VMEM((w, BRANCH_W), BF16),
        ],
        compiler_params=_cparams(2),
        name="mixer_prompt",
    )(h, tabs["cos"], tabs["sin"], tabs["dmat"], tabs["qdec"], tabs["kdec"], tabs["blk"], tabs["smask"],
      conv_w, ln_g3, ln_b3, ws, bst, *([bias] * n_blk))


def _mixer_sample_body(h_ref, cos_ref, sin_ref, dmat_ref, qdec_ref, kdec_ref, blk_ref, smask_ref,
                       convw_ref, lng_ref, lnb_ref, ws_ref, bst_ref, bias_ref,
                       sret_in_ref, sconv_in_ref, ck_ref, cv_ref,
                       br_ref, sret_ref, sconv_ref, ko_ref, vo_ref, gv_ref, *, length, group):
    vn = _gmlp(h_ref, lng_ref[0], lnb_ref[0], ws_ref, bst_ref, br_ref, length)
    for g in range(group):
        gv_ref[g] = vn[g * length:(g + 1) * length]

    for g in range(group):
        hg = h_ref.at[g * length:(g + 1) * length]
        brg = br_ref.at[g * length:(g + 1) * length]

        states = _retention(hg, cos_ref[...], sin_ref[...], dmat_ref, qdec_ref[...], kdec_ref[...], blk_ref,
                            smask_ref[...], _state_to_pairs(sret_in_ref.at[0, g]), brg)
        _pairs_to_state(states, sret_ref.at[g])

        prev8 = jnp.concatenate([jnp.zeros((8 - (CONV_W - 1), BRANCH_W), F32), sconv_in_ref[0, g]], axis=0)
        z = _conv(hg, prev8, convw_ref[0], brg)
        sconv_ref[g] = z[length - (CONV_W - 1):length]

        ak = hg[:, C_AK:C_AK + BRANCH_W]
        av = hg[:, C_AV:C_AV + BRANCH_W]
        ko_ref[g] = ak.astype(F32)
        vo_ref[g] = av.astype(F32)
        _attention_sample(hg, ck_ref[0, g].astype(BF16), cv_ref[0, g].astype(BF16), ak, av, bias_ref, brg)


def _mixer_sample(h, tabs, conv_w, ln_g3, ln_b3, ws, bst, bias, state_ret, state_conv, cache_k, cache_v,
                  layer, streams, length, group):
    full = lambda shape: pl.BlockSpec(shape, lambda s: (0,) * len(shape))
    per_layer = lambda shape: pl.BlockSpec((1,) + shape, lambda s: (layer,) + (0,) * len(shape))
    per_stream = lambda shape: pl.BlockSpec((1, group) + shape, lambda s: (layer, s) + (0,) * len(shape))
    out_stream = lambda shape: pl.BlockSpec((group,) + shape, lambda s: (s,) + (0,) * len(shape))
    w = bias.shape[-1]
    return pl.pallas_call(
        functools.partial(_mixer_sample_body, length=length, group=group),
        grid=(streams // group,),
        in_specs=[
            pl.BlockSpec((group * length, MIX_COLS), lambda s: (s, 0)),
            full((length, 256)), full((length, 256)),
            full((RET_HEADS, length, length)),
            full((length, 256)), full((length, 256)),
            full((2, 1, 256)),
            full((2 * RET_DK, 2 * RET_DV)),
            per_layer((CONV_W, BRANCH_W)),
            per_layer((1, BRANCH_W)), per_layer((1, BRANCH_W)),
            per_layer((GMLP_GROUPS, GMLP_BLOCK, GMLP_BLOCK)),
            per_layer((GMLP_BLOCK, GMLP_GROUPS)),
            per_layer((1, ATT_HEADS, length, w)),
            per_stream((RET_HEADS, RET_DK, RET_DV)),
            per_stream((CONV_W - 1, BRANCH_W)),
            per_stream((BRANCH_W, ATT_BAND)),
            per_stream((BRANCH_W, ATT_BAND)),
        ],
        out_specs=[
            pl.BlockSpec((group * length, D_MODEL), lambda s: (s, 0)),
            out_stream((RET_HEADS, RET_DK, RET_DV)),
            out_stream((CONV_W - 1, BRANCH_W)),
            out_stream((length, BRANCH_W)),
            out_stream((length, BRANCH_W)),
            out_stream((length, BRANCH_W)),
        ],
        out_shape=[
            jax.ShapeDtypeStruct((streams * length, D_MODEL), BF16),
            jax.ShapeDtypeStruct((streams, RET_HEADS, RET_DK, RET_DV), F32),
            jax.ShapeDtypeStruct((streams, CONV_W - 1, BRANCH_W), F32),
            jax.ShapeDtypeStruct((streams, length, BRANCH_W), F32),
            jax.ShapeDtypeStruct((streams, length, BRANCH_W), F32),
            jax.ShapeDtypeStruct((streams, length, BRANCH_W), F32),
        ],
        compiler_params=_cparams(1),
        name="mixer_sample",
    )(h, tabs["cos"], tabs["sin"], tabs["dmat"], tabs["qdec"], tabs["kdec"], tabs["blk"], tabs["smask"],
      conv_w, ln_g3, ln_b3, ws, bst, bias, state_ret, state_conv, cache_k, cache_v)


def _mixer_tables(positions, blk_len):
    half = RET_DK // 2
    freqs = ROPE_BASE ** (-jnp.arange(half, dtype=F32) / half)
    ang = jnp.asarray(positions, F32)[:, None] * freqs[None, :]
    cos, sin = jnp.cos(ang), jnp.sin(ang)
    cos_t = jnp.tile(cos, (1, 2 * RET_HEADS))
    sin_t = jnp.tile(jnp.concatenate([-sin, sin], axis=1), (1, RET_HEADS))
    lg = jnp.log1p(-jnp.exp2(-5.0 - jnp.arange(RET_HEADS, dtype=F32)))
    t = jnp.arange(blk_len, dtype=F32)
    diff = t[:, None] - t[None, :]
    dmat = jnp.where(diff >= 0, jnp.exp(jnp.maximum(diff, 0.0)[None] * lg[:, None, None]), 0.0)
    qdec = jnp.repeat(jnp.exp((t[:, None] + 1.0) * lg[None, :]), RET_DK, axis=1)
    kdec = jnp.repeat(jnp.exp((blk_len - 1.0 - t)[:, None] * lg[None, :]), RET_DK, axis=1)
    blk = jnp.repeat(jnp.exp(blk_len * lg), RET_DV).reshape(RET_HEADS // 2, 1, 2 * RET_DV)
    smask = jnp.kron(jnp.eye(2, dtype=F32), jnp.ones((RET_DK, RET_DV), F32))
    return dict(cos=cos_t, sin=sin_t, dmat=dmat, qdec=qdec, kdec=kdec, blk=blk, smask=smask)


def _pick_tile(n, target):
    t = min(n, target)
    while n % t:
        t //= 2
    return t


def kernel(x_prompt, x_sample, state_ret, state_conv, cache_att_k, cache_att_v, norm_gain, w_in, w_branch, w_out,
           conv_w, gmlp_ln_gain, gmlp_ln_bias, gmlp_ws, gmlp_bs, att_rel_bias, final_norm_gain):
    batch, seq, _ = x_prompt.shape
    streams, dec_len, _ = x_sample.shape
    depth = w_in.shape[0]
    assert seq % MIXER_TT == 0 and seq >= ATT_BAND and cache_att_k.shape[2] == ATT_BAND

    tt = MIXER_TT
    n_p, n_s = batch * seq, streams * dec_len
    tm_p, tm_s = _pick_tile(n_p, 1024), _pick_tile(n_s, 512)

    w_mix_t, w_merge_t = _pack_weights(w_in, w_branch, w_out)
    gain3 = norm_gain.reshape(depth, 1, D_MODEL)
    ln_g3 = gmlp_ln_gain.reshape(depth, 1, BRANCH_W)
    ln_b3 = gmlp_ln_bias.reshape(depth, 1, BRANCH_W)
    bst = jnp.swapaxes(gmlp_bs, 1, 2)
    table_pad = jnp.pad(att_rel_bias, ((0, 0), (0, 0), (0, TABLE_LANES - att_rel_bias.shape[-1])))
    to_t = lambda c: jnp.transpose(c, (0, 1, 3, 4, 2)).reshape(depth, streams, BRANCH_W, ATT_BAND)
    cache_k, cache_v = to_t(cache_att_k), to_t(cache_att_v)
    gain2 = final_norm_gain.reshape(1, D_MODEL)

    tabs_p = _mixer_tables(np.arange(seq), tt)
    tabs_s = _mixer_tables(PAST_LEN + np.arange(dec_len), dec_len)

    hp = x_prompt.reshape(n_p, D_MODEL)
    hs = x_sample.reshape(n_s, D_MODEL)
    outs = [[] for _ in range(9)]
    w_p = ATT_BAND + ATT_ROWS
    bias_p = _rel_bias(table_pad, ATT_BAND // ATT_ROWS + 1, ATT_ROWS, w_p, w_p, True)
    w_s = -(-(ATT_BAND + dec_len) // LANES) * LANES
    bias_s = _rel_bias(table_pad, 1, dec_len, w_s, ATT_BAND + dec_len, False)
    for l in range(depth):
        h_mix, xn = _inproj(hp, gain3, w_mix_t, l, tm_p, INPROJ_TN)
        br, p_ret, p_conv, p_k, p_v = _mixer_prompt(h_mix, tabs_p, conv_w, ln_g3, ln_b3, gmlp_ws, bst, bias_p,
                                                    l, batch, seq, tt)
        hp = _merge(xn, br, hp, w_merge_t, gain2, l, tm_p, l == depth - 1)

        h_mix, xn = _inproj(hs, gain3, w_mix_t, l, tm_s, INPROJ_TN)
        br, s_ret, s_conv, s_k, s_v, s_gv = _mixer_sample(h_mix, tabs_s, conv_w, ln_g3, ln_b3, gmlp_ws, bst, bias_s,
                                                          state_ret, state_conv, cache_k, cache_v,
                                                          l, streams, dec_len, _pick_tile(streams, SAMPLE_GROUP))
        hs = _merge(xn, br, hs, w_merge_t, gain2, l, tm_s, l == depth - 1)

        for dst, val in zip(outs, (p_ret, p_conv, p_k, p_v, s_ret, s_conv, s_k, s_v, s_gv)):
            dst.append(val)

    y_prompt = hp.reshape(batch, seq, D_MODEL)
    y_sample = hs.reshape(streams, dec_len, D_MODEL)
    p_ret, p_conv, p_k, p_v, s_ret, s_conv, s_k, s_v, s_gv = [jnp.stack(o) for o in outs]
    kv_shape = lambda a: a.reshape(a.shape[:3] + (ATT_HEADS, ATT_DH))
    kv_from_t = lambda a: jnp.transpose(a.reshape(a.shape[:2] + (ATT_HEADS, ATT_DH, ATT_BAND)), (0, 1, 4, 2, 3))
    return (y_prompt, y_sample, p_ret, p_conv, kv_from_t(p_k), kv_from_t(p_v),
            s_ret, s_conv, kv_shape(s_k), kv_shape(s_v), s_gv)
```

```python
import functools

import numpy as np
import jax
import jax.numpy as jnp
from jax import lax
from jax.experimental import pallas as pl
from jax.experimental.pallas import tpu as pltpu

F32 = jnp.float32
BF16 = jnp.bfloat16

D_MODEL = 2048
BRANCH_W = 512
N_BRANCH = 4
CHUNK = 64
RET_HEADS = 4
RET_DK = 64
RET_DV = 128
ROPE_BASE = 10000.0
CONV_W = 3
GMLP_BLOCK = 128
GMLP_GROUPS = 4
ATT_HEADS = 8
ATT_DH = 64
ATT_BAND = 512
MAX_REL = 128
NORM_EPS = 1e-6
PAST_LEN = 2048
NEG_BIG = -1e30

MIX_COLS = 2 * RET_HEADS * RET_DK + 13 * BRANCH_W
IN_COLS = MIX_COLS + N_BRANCH * D_MODEL
C_RQ, C_RK, C_RV, C_RG = 0, 256, 512, 1024
C_CB, C_CC, C_CX, C_CG = 1536, 2048, 2560, 3072
C_MU, C_MV, C_MG = 3584, 4096, 4608
C_AQ, C_AK, C_AV, C_AG = 5120, 5632, 6144, 6656

LANES = 128
WBLK = 256
INPROJ_TN = 7 * WBLK
ATT_ROWS = 256
MIXER_TT = 256
MERGE_ROW_CHUNK = 256
NORM_ROW_CHUNK = 256
SAMPLE_GROUP = 4
SAMPLE_W_SLOTS = 2
LOG2E = 1.4426950408889634
VMEM_LIMIT_BYTES = 56 * 1024 * 1024
TABLE_LANES = 384


def _cparams(n_axes):
    return pltpu.CompilerParams(dimension_semantics=("arbitrary",) * n_axes,
                                vmem_limit_bytes=VMEM_LIMIT_BYTES)


def _dot(a, b):
    return jnp.dot(a, b, preferred_element_type=F32)


def _dot_nt(a, b):
    return lax.dot_general(a, b, (((1,), (1,)), ((), ())), preferred_element_type=F32)


def _dot_tn(a, b):
    return lax.dot_general(a, b, (((0,), (0,)), ((), ())), preferred_element_type=F32)


def _silu(x):
    return x * jax.nn.sigmoid(x)


def _inproj_body(x_ref, g_ref, *rest, nsub, w_slots):
    w_refs, (h_ref, xn_ref) = rest[:w_slots], rest[w_slots:]
    j = pl.program_id(1)
    tm = x_ref.shape[0]
    rc = min(tm, NORM_ROW_CHUNK)
    blocks = [(k * nsub + c, w_refs[k].at[0, c]) for k in range(w_slots) for c in range(nsub)]

    @pl.when(j == 0)
    def _():
        for r0 in range(0, tm, rc):
            x = x_ref[r0:r0 + rc, :]
            inv = lax.rsqrt(jnp.mean(x * x, axis=-1, keepdims=True) + NORM_EPS)
            xn = (x * inv * g_ref[0]).astype(BF16)
            xn_ref[r0:r0 + rc, :] = xn
            for c, w in blocks:
                h_ref[r0:r0 + rc, c * WBLK:(c + 1) * WBLK] = _dot(xn, w[...]).astype(h_ref.dtype)

    @pl.when(j > 0)
    def _():
        xn = xn_ref[...]
        for c, w in blocks:
            h_ref[:, c * WBLK:(c + 1) * WBLK] = _dot(xn, w[...]).astype(h_ref.dtype)


def _inproj(x2d, gain3, w_mix_t, layer, tm, tn, w_slots):
    n = x2d.shape[0]
    nsub = tn // WBLK
    w_spec = lambda k: pl.BlockSpec((1, nsub, D_MODEL, WBLK), lambda i, j: (layer, w_slots * j + k, 0, 0))
    return pl.pallas_call(
        functools.partial(_inproj_body, nsub=nsub, w_slots=w_slots),
        grid=(n // tm, MIX_COLS // (tn * w_slots)),
        in_specs=[
            pl.BlockSpec((tm, D_MODEL), lambda i, j: (i, 0)),
            pl.BlockSpec((1, 1, D_MODEL), lambda i, j: (layer, 0, 0)),
        ] + [w_spec(k) for k in range(w_slots)],
        out_specs=[
            pl.BlockSpec((tm, tn * w_slots), lambda i, j: (i, j)),
            pl.BlockSpec((tm, D_MODEL), lambda i, j: (i, 0)),
        ],
        out_shape=[
            jax.ShapeDtypeStruct((n, MIX_COLS), BF16),
            jax.ShapeDtypeStruct((n, D_MODEL), BF16),
        ],
        compiler_params=_cparams(2),
        name="inproj",
    )(x2d, gain3, *([w_mix_t] * w_slots))


MERGE_ROWS = N_BRANCH * D_MODEL + N_BRANCH * BRANCH_W + D_MODEL
MERGE_WO_ROW0 = N_BRANCH * D_MODEL + N_BRANCH * BRANCH_W


def _add_out_projection(hm_sc, wo_ref, out_ref, first):
    hm = hm_sc[...]
    for j in range(D_MODEL // WBLK):
        upd = _dot(hm, wo_ref[j * WBLK:(j + 1) * WBLK, :])
        if first and j > 0:
            out_ref[:, j * WBLK:(j + 1) * WBLK] = upd
        else:
            out_ref[:, j * WBLK:(j + 1) * WBLK] += upd


def _merge_body(xn_ref, br_ref, x_ref, *rest, final_norm, w_slots):
    w_refs, rest = rest[:w_slots], rest[w_slots:]
    fg_ref, out_ref, hm_sc, wlast_sc = rest if final_norm else (None,) + rest
    s = pl.program_id(1)
    nb = D_MODEL // WBLK
    tm = out_ref.shape[0]
    slot_rows = MERGE_ROWS // w_slots

    def wrows(r0, r1):
        k = r0 // slot_rows
        assert (r1 - 1) // slot_rows == k
        return w_refs[k].at[0, 0, r0 - k * slot_rows:r1 - k * slot_rows]

    wo_ref = wrows(MERGE_WO_ROW0, MERGE_ROWS)

    @pl.when(s == 0)
    def _():
        out_ref[:, 0:WBLK] = x_ref[...]
        wlast_sc[...] = wo_ref[...]

    @pl.when(s == 1)
    def _():
        _add_out_projection(hm_sc, wo_ref, out_ref, True)

    @pl.when(s > 1)
    def _():
        _add_out_projection(hm_sc, wo_ref, out_ref, False)

    rc = min(tm, MERGE_ROW_CHUNK)
    for r in range(tm // rc):
        rows = slice(r * rc, (r + 1) * rc)
        xn = xn_ref[rows, :]
        acc = None
        for i in range(N_BRANCH):
            logits = _dot(xn, wrows(i * D_MODEL, (i + 1) * D_MODEL)[...])
            wb0 = N_BRANCH * D_MODEL + i * BRANCH_W
            proj = _dot(br_ref[rows, i * BRANCH_W:(i + 1) * BRANCH_W], wrows(wb0, wb0 + BRANCH_W)[...])
            term = jax.nn.sigmoid(logits) * proj
            acc = term if acc is None else acc + term
        hm_sc[rows, :] = acc.astype(BF16)

    @pl.when(s == nb - 1)
    def _():
        _add_out_projection(hm_sc, wlast_sc, out_ref, False)

    for c in range(1, nb):
        @pl.when(s == c)
        def _():
            out_ref[:, c * WBLK:(c + 1) * WBLK] += x_ref[...]

    if final_norm:
        @pl.when(s == nb - 1)
        def _():
            y = out_ref[...]
            r = lax.rsqrt(jnp.mean(y * y, axis=-1, keepdims=True) + NORM_EPS)
            out_ref[...] = y * r * fg_ref[...]


def _merge(xn, br, x2d, w_merge_t, final_gain2, layer, tm, final_norm, w_slots):
    n = x2d.shape[0]
    nb = D_MODEL // WBLK
    w_spec = lambda k: pl.BlockSpec((1, 1, MERGE_ROWS // w_slots, WBLK), lambda m, s: (layer, s, k, 0))
    in_specs = [
        pl.BlockSpec((tm, D_MODEL), lambda m, s: (m, 0)),
        pl.BlockSpec((tm, D_MODEL), lambda m, s: (m, 0)),
        pl.BlockSpec((tm, WBLK), lambda m, s: (m, s)),
    ] + [w_spec(k) for k in range(w_slots)]
    operands = [xn, br, x2d] + [w_merge_t] * w_slots
    if final_norm:
        in_specs.append(pl.BlockSpec((1, D_MODEL), lambda m, s: (0, 0)))
        operands.append(final_gain2)
    return pl.pallas_call(
        functools.partial(_merge_body, final_norm=final_norm, w_slots=w_slots),
        grid=(n // tm, nb),
        in_specs=in_specs,
        out_specs=pl.BlockSpec((tm, D_MODEL), lambda m, s: (m, 0)),
        out_shape=jax.ShapeDtypeStruct((n, D_MODEL), F32),
        scratch_shapes=[pltpu.VMEM((tm, WBLK), BF16), pltpu.VMEM((D_MODEL, WBLK), BF16)],
        compiler_params=_cparams(2),
        name="merge",
    )(*operands)


def _pack_mix_body(w_ref, o_ref, *, nsub):
    for c in range(nsub):
        o_ref[0, c] = w_ref[0, :, c * WBLK:(c + 1) * WBLK].astype(BF16)


def _pack_merge_body(g0_ref, g1_ref, g2_ref, g3_ref, wb_ref, wo_ref, o_ref):
    for i, g_ref in enumerate((g0_ref, g1_ref, g2_ref, g3_ref)):
        o_ref[0, 0, i * D_MODEL:(i + 1) * D_MODEL, :] = g_ref[0].astype(BF16)
    r0 = N_BRANCH * D_MODEL
    o_ref[0, 0, r0:r0 + N_BRANCH * BRANCH_W, :] = wb_ref[0].astype(BF16)
    r0 += N_BRANCH * BRANCH_W
    for j in range(D_MODEL // WBLK):
        o_ref[0, 0, r0 + j * WBLK:r0 + (j + 1) * WBLK, :] = wo_ref[0, :, j * WBLK:(j + 1) * WBLK].astype(BF16)


def _pack_weights(w_in, w_branch, w_out):
    depth = w_in.shape[0]
    nb = D_MODEL // WBLK
    nsub = INPROJ_TN // WBLK
    w_mix_t = pl.pallas_call(
        functools.partial(_pack_mix_body, nsub=nsub),
        grid=(depth, MIX_COLS // INPROJ_TN),
        in_specs=[pl.BlockSpec((1, D_MODEL, INPROJ_TN), lambda l, j: (l, 0, j))],
        out_specs=pl.BlockSpec((1, nsub, D_MODEL, WBLK), lambda l, j: (l, j, 0, 0)),
        out_shape=jax.ShapeDtypeStruct((depth, MIX_COLS // WBLK, D_MODEL, WBLK), BF16),
        compiler_params=_cparams(2),
        name="pack_mix",
    )(w_in)

    gate_blk0 = MIX_COLS // WBLK
    gate_spec = lambda i: pl.BlockSpec((1, D_MODEL, WBLK), lambda l, c: (l, 0, gate_blk0 + i * nb + c))
    w_merge_t = pl.pallas_call(
        _pack_merge_body,
        grid=(depth, nb),
        in_specs=[
            gate_spec(0), gate_spec(1), gate_spec(2), gate_spec(3),
            pl.BlockSpec((1, N_BRANCH * BRANCH_W, WBLK), lambda l, c: (l, 0, c)),
            pl.BlockSpec((1, WBLK, D_MODEL), lambda l, c: (l, (c + nb - 1) % nb, 0)),
        ],
        out_specs=pl.BlockSpec((1, 1, MERGE_ROWS, WBLK), lambda l, c: (l, c, 0, 0)),
        out_shape=jax.ShapeDtypeStruct((depth, nb, MERGE_ROWS, WBLK), BF16),
        compiler_params=_cparams(2),
        name="pack_merge",
    )(w_in, w_in, w_in, w_in, w_branch.reshape(depth, N_BRANCH * BRANCH_W, D_MODEL), w_out)
    return w_mix_t, w_merge_t


def _bias_body(tab_ref, perm_ref, out_ref, *, n_var, tq, w, w_valid, wpad, banded):
    t = tab_ref[0] * LOG2E
    perm = perm_ref[...]
    t_hi = t.astype(BF16)
    r1 = t - t_hi.astype(F32)
    t_mid = r1.astype(BF16)
    t_lo = (r1 - t_mid.astype(F32)).astype(BF16)
    row0 = _dot(t_hi, perm) + _dot(t_mid, perm) + _dot(t_lo, perm)
    qi = lax.broadcasted_iota(jnp.int32, (tq, w), 0)
    kj = lax.broadcasted_iota(jnp.int32, (tq, w), 1)
    visible = kj < w_valid
    if banded:
        lo = (qi // CHUNK) * CHUNK
        visible = visible & (kj >= lo) & (kj < lo + ATT_BAND + CHUNK)
    for h in range(ATT_HEADS):
        full = jnp.broadcast_to(row0[h:h + 1, :], (tq, wpad))
        shifted = pltpu.roll(full, 0, 1, stride=1, stride_axis=0)[:, :w]
        for u in range(n_var):
            vis_u = visible & (kj >= ATT_BAND - u * tq) if banded else visible
            out_ref[0, u, h] = jnp.where(vis_u, shifted, NEG_BIG)


def _bias_perm(w, wpad):
    j = np.arange(wpad)
    m = np.clip(ATT_BAND - j, -MAX_REL, MAX_REL) + MAX_REL
    m = np.where(j >= w, 2 * MAX_REL, m)
    perm = np.zeros((TABLE_LANES, wpad), np.float32)
    perm[m, j] = 1.0
    return jnp.asarray(perm, BF16)


def _rel_bias(table_pad, n_var, tq, w, w_valid, banded):
    depth = table_pad.shape[0]
    wpad = -(-(w + tq) // LANES) * LANES
    return pl.pallas_call(
        functools.partial(_bias_body, n_var=n_var, tq=tq, w=w, w_valid=w_valid, wpad=wpad, banded=banded),
        grid=(depth,),
        in_specs=[
            pl.BlockSpec((1, ATT_HEADS, TABLE_LANES), lambda l: (l, 0, 0)),
            pl.BlockSpec((TABLE_LANES, wpad), lambda l: (0, 0)),
        ],
        out_specs=pl.BlockSpec((1, n_var, ATT_HEADS, tq, w), lambda l: (l, 0, 0, 0, 0)),
        out_shape=jax.ShapeDtypeStruct((depth, n_var, ATT_HEADS, tq, w), F32),
        compiler_params=_cparams(1),
        name="rel_bias",
    )(table_pad, _bias_perm(w, wpad))


def _rotary(x, cos, sin_signed):
    lane = lax.broadcasted_iota(jnp.int32, x.shape, 1)
    first_half = (lane % RET_DK) < (RET_DK // 2)
    width = x.shape[1]
    swapped = jnp.where(first_half, pltpu.roll(x, width - RET_DK // 2, 1), pltpu.roll(x, RET_DK // 2, 1))
    return x * cos + swapped * sin_signed


def _retention(h_ref, cos, sin_signed, dmat_ref, qdec, kdec, blk_ref, smask, states, br_ref):
    length = h_ref.shape[0]
    q = _rotary(h_ref[:, C_RQ:C_RQ + 256].astype(F32), cos, sin_signed)
    k = _rotary(h_ref[:, C_RK:C_RK + 256].astype(F32), cos, sin_signed) * (RET_DK ** -0.5)
    kb = k.astype(BF16)
    qd, kd = (q * qdec).astype(BF16), (k * kdec).astype(BF16)
    first = lax.broadcasted_iota(jnp.int32, (length, LANES), 1) < RET_DK
    new_states = []
    for p in range(RET_HEADS // 2):
        cols = slice(p * LANES, (p + 1) * LANES)
        sp = states[p]
        v_pair = h_ref[:, C_RV + p * 2 * RET_DV:C_RV + (p + 1) * 2 * RET_DV]
        o_inter = _dot(qd[:, cols], sp.astype(BF16))
        q2 = jnp.concatenate([jnp.where(first, q[:, cols], 0.0), jnp.where(first, 0.0, q[:, cols])],
                             axis=0).astype(BF16)
        decay2 = dmat_ref[2 * p:2 * p + 2].reshape(2 * length, length)
        inner = (_dot_nt(q2, kb[:, cols]) * decay2).astype(BF16)
        o2 = _dot(inner, v_pair)
        for half in range(2):
            hh = 2 * p + half
            vl = slice(half * RET_DV, (half + 1) * RET_DV)
            o = o2[half * length:(half + 1) * length, vl] + o_inter[:, vl]
            mu = jnp.mean(o, axis=-1, keepdims=True)
            oc = o - mu
            var = jnp.mean(oc * oc, axis=-1, keepdims=True)
            hn = oc * lax.rsqrt(var + NORM_EPS)
            gcols = slice(C_RG + hh * RET_DV, C_RG + (hh + 1) * RET_DV)
            br_ref[:, hh * RET_DV:(hh + 1) * RET_DV] = (_silu(h_ref[:, gcols].astype(F32)) * hn).astype(BF16)
        upd = _dot_tn(kd[:, cols], v_pair)
        new_states.append((sp * blk_ref[p] + upd) * smask)
    return new_states


def _conv(h_ref, prev8, convw, br_ref):
    z = h_ref[:, C_CC:C_CC + BRANCH_W].astype(F32) * h_ref[:, C_CX:C_CX + BRANCH_W].astype(F32)
    zcat = jnp.concatenate([prev8, z], axis=0)
    z1 = pltpu.roll(zcat, 1, 0)[8:]
    z2 = pltpu.roll(zcat, 2, 0)[8:]
    y = convw[0:1] * z2 + convw[1:2] * z1 + convw[2:3] * z
    cb = h_ref[:, C_CB:C_CB + BRANCH_W].astype(F32)
    cg = h_ref[:, C_CG:C_CG + BRANCH_W].astype(F32)
    br_ref[:, BRANCH_W:2 * BRANCH_W] = (_silu(cg) * (cb * y)).astype(BF16)
    return z


def _gmlp(h_ref, lng, lnb, ws_ref, bst_ref, br_ref, blk_len):
    length = h_ref.shape[0]
    mv = h_ref[:, C_MV:C_MV + BRANCH_W].astype(F32)
    mean = jnp.mean(mv, axis=-1, keepdims=True)
    cen = mv - mean
    var = jnp.mean(cen * cen, axis=-1, keepdims=True)
    vn = cen * lax.rsqrt(var + NORM_EPS) * lng + lnb
    vnb = vn.astype(BF16)
    ri = lax.broadcasted_iota(jnp.int32, (blk_len, blk_len), 0)
    ci = lax.broadcasted_iota(jnp.int32, (blk_len, blk_len), 1)
    causal = (ci // CHUNK) <= (ri // CHUNK)
    n_blocks = length // blk_len
    for g in range(GMLP_GROUPS):
        wg = jnp.where(causal, ws_ref[0, g, 0:blk_len, 0:blk_len], 0.0).astype(BF16)
        bcol = bst_ref[0, 0:blk_len, g:g + 1]
        gc = slice(g * LANES, (g + 1) * LANES)
        side = jnp.concatenate([vnb[n * blk_len:(n + 1) * blk_len, gc] for n in range(n_blocks)], axis=1)
        mix_all = _dot(wg, side) + bcol
        for n in range(n_blocks):
            rows = slice(n * blk_len, (n + 1) * blk_len)
            mix = mix_all[:, n * LANES:(n + 1) * LANES]
            mu = h_ref[rows, C_MU + g * LANES:C_MU + (g + 1) * LANES].astype(F32)
            mg = h_ref[rows, C_MG + g * LANES:C_MG + (g + 1) * LANES].astype(F32)
            br_ref[rows, 2 * BRANCH_W + g * LANES:2 * BRANCH_W + (g + 1) * LANES] = (
                _silu(mg) * (mu * mix)).astype(BF16)
    return vn


def _attention(h_ref, kh_ref, vh_ref, bias_ref, br_ref, r0, rows, c0, width):
    lane = lax.broadcasted_iota(jnp.int32, (rows, LANES), 1)
    rsl = slice(r0, r0 + rows)
    for p in range(ATT_HEADS // 2):
        cols = slice(p * LANES, (p + 1) * LANES)
        qp = h_ref[rsl, C_AQ + p * LANES:C_AQ + (p + 1) * LANES].astype(F32) * (LOG2E * ATT_DH ** -0.5)
        kp = kh_ref[c0:c0 + width, cols]
        vp = vh_ref[c0:c0 + width, cols]
        first = lane < ATT_DH
        q2 = jnp.concatenate([jnp.where(first, qp, 0.0), jnp.where(first, 0.0, qp)], axis=0).astype(BF16)
        s = _dot_nt(q2, kp) + bias_ref[0, 2 * p:2 * p + 2].reshape(2 * rows, width)
        m = jnp.max(s, axis=-1, keepdims=True)
        e = jnp.exp2(s - m)
        denom = jnp.sum(e, axis=-1, keepdims=True)
        o = _dot(e.astype(BF16), vp) / denom
        acc = jnp.where(first, o[0:rows], o[rows:2 * rows])
        ag = h_ref[rsl, C_AG + p * LANES:C_AG + (p + 1) * LANES].astype(F32)
        br_ref[rsl, 3 * BRANCH_W + p * LANES:3 * BRANCH_W + (p + 1) * LANES] = (_silu(ag) * acc).astype(BF16)


def _attention_sample(h_ref, kt, vt, ak, av, bias_ref, br_ref):
    length = h_ref.shape[0]
    n_old = kt.shape[1]
    lane_head = lax.broadcasted_iota(jnp.int32, (length, BRANCH_W), 1) // ATT_DH
    q = h_ref[:, C_AQ:C_AQ + BRANCH_W].astype(F32) * (LOG2E * ATT_DH ** -0.5)
    q_st = jnp.concatenate([jnp.where(lane_head == hh, q, 0.0) for hh in range(ATT_HEADS)], axis=0).astype(BF16)
    bias = bias_ref[0, 0].reshape(ATT_HEADS * length, bias_ref.shape[-1])
    s_old = _dot(q_st, kt) + bias[:, 0:n_old]
    s_new = _dot_nt(q_st, ak) + bias[:, n_old:n_old + length]
    m = jnp.maximum(jnp.max(s_old, axis=-1, keepdims=True), jnp.max(s_new, axis=-1, keepdims=True))
    e_old = jnp.exp2(s_old - m)
    e_new = jnp.exp2(s_new - m)
    denom = jnp.sum(e_old, axis=-1, keepdims=True) + jnp.sum(e_new, axis=-1, keepdims=True)
    o_st = (_dot_nt(e_old.astype(BF16), vt) + _dot(e_new.astype(BF16), av)) / denom
    o = None
    for hh in range(ATT_HEADS):
        part = jnp.where(lane_head == hh, o_st[hh * length:(hh + 1) * length], 0.0)
        o = part if o is None else o + part
    ag = h_ref[:, C_AG:C_AG + BRANCH_W].astype(F32)
    br_ref[:, 3 * BRANCH_W:4 * BRANCH_W] = (_silu(ag) * o).astype(BF16)


def _state_to_pairs(state_ref):
    pairs = []
    zero = jnp.zeros((RET_DK, RET_DV), F32)
    for p in range(RET_HEADS // 2):
        top = jnp.concatenate([state_ref[2 * p], zero], axis=1)
        bot = jnp.concatenate([zero, state_ref[2 * p + 1]], axis=1)
        pairs.append(jnp.concatenate([top, bot], axis=0))
    return pairs


def _pairs_to_state(pairs, out_ref):
    for p in range(RET_HEADS // 2):
        out_ref[2 * p] = pairs[p][0:RET_DK, 0:RET_DV]
        out_ref[2 * p + 1] = pairs[p][RET_DK:2 * RET_DK, RET_DV:2 * RET_DV]


def _mixer_prompt_body(h_ref, cos_ref, sin_ref, dmat_ref, qdec_ref, kdec_ref, blk_ref, smask_ref,
                       convw_ref, lng_ref, lnb_ref, ws_ref, bst_ref, *rest, tt, t_keep):
    n_blk = tt // ATT_ROWS
    bias_refs = rest[:n_blk]
    br_ref, sret_ref, sconv_ref, ko_ref, vo_ref, sp_sc, zc_sc, kh_sc, vh_sc = rest[n_blk:]
    t = pl.program_id(1)

    @pl.when(t == 0)
    def _():
        sp_sc[...] = jnp.zeros_like(sp_sc)
        zc_sc[...] = jnp.zeros_like(zc_sc)
        kh_sc[0:ATT_BAND] = jnp.zeros((ATT_BAND, BRANCH_W), BF16)
        vh_sc[0:ATT_BAND] = jnp.zeros((ATT_BAND, BRANCH_W), BF16)

    states = _retention(h_ref, cos_ref[...], sin_ref[...], dmat_ref, qdec_ref[...], kdec_ref[...], blk_ref,
                        smask_ref[...], [sp_sc[0], sp_sc[1]], br_ref)
    sp_sc[0] = states[0]
    sp_sc[1] = states[1]
    _pairs_to_state(states, sret_ref.at[0])

    z = _conv(h_ref, zc_sc[...], convw_ref[0], br_ref)
    tail = z[tt - (CONV_W - 1):tt]
    zc_sc[8 - (CONV_W - 1):8] = tail
    sconv_ref[0] = tail

    _gmlp(h_ref, lng_ref[0], lnb_ref[0], ws_ref, bst_ref, br_ref, GMLP_BLOCK)

    ak = h_ref[:, C_AK:C_AK + BRANCH_W]
    av = h_ref[:, C_AV:C_AV + BRANCH_W]
    kh_sc[ATT_BAND:ATT_BAND + tt] = ak
    vh_sc[ATT_BAND:ATT_BAND + tt] = av
    for blk, bias_ref in enumerate(bias_refs):
        _attention(h_ref, kh_sc, vh_sc, bias_ref.at[0], br_ref, blk * ATT_ROWS, ATT_ROWS, blk * ATT_ROWS,
                   ATT_BAND + ATT_ROWS)
    k_keep = kh_sc[tt:tt + ATT_BAND]
    v_keep = vh_sc[tt:tt + ATT_BAND]
    kh_sc[0:ATT_BAND] = k_keep
    vh_sc[0:ATT_BAND] = v_keep

    @pl.when(t >= t_keep)
    def _():
        ko_ref[0] = ak.astype(F32).T
        vo_ref[0] = av.astype(F32).T


def _mixer_prompt(h, tabs, conv_w, ln_g3, ln_b3, ws, bst, bias, layer, batch, seq, tt):
    nt = seq // tt
    t_keep = (seq - ATT_BAND) // tt
    full = lambda shape: pl.BlockSpec(shape, lambda b, t: (0,) * len(shape))
    per_layer = lambda shape: pl.BlockSpec((1,) + shape, lambda b, t: (layer,) + (0,) * len(shape))
    w = ATT_BAND + tt
    n_blk = tt // ATT_ROWS
    n_var = bias.shape[1]
    bias_spec = lambda blk: pl.BlockSpec((1, 1, ATT_HEADS, ATT_ROWS, ATT_BAND + ATT_ROWS),
                                         lambda b, t: (layer, jnp.minimum(n_blk * t + blk, n_var - 1), 0, 0, 0))
    return pl.pallas_call(
        functools.partial(_mixer_prompt_body, tt=tt, t_keep=t_keep),
        grid=(batch, nt),
        in_specs=[
            pl.BlockSpec((tt, MIX_COLS), lambda b, t: (b * nt + t, 0)),
            pl.BlockSpec((tt, 256), lambda b, t: (t, 0)),
            pl.BlockSpec((tt, 256), lambda b, t: (t, 0)),
            full((RET_HEADS, tt, tt)),
            full((tt, 256)), full((tt, 256)),
            full((2, 1, 256)),
            full((2 * RET_DK, 2 * RET_DV)),
            per_layer((CONV_W, BRANCH_W)),
            per_layer((1, BRANCH_W)), per_layer((1, BRANCH_W)),
            per_layer((GMLP_GROUPS, GMLP_BLOCK, GMLP_BLOCK)),
            per_layer((GMLP_BLOCK, GMLP_GROUPS)),
        ] + [bias_spec(blk) for blk in range(n_blk)],
        out_specs=[
            pl.BlockSpec((tt, D_MODEL), lambda b, t: (b * nt + t, 0)),
            pl.BlockSpec((1, RET_HEADS, RET_DK, RET_DV), lambda b, t: (b, 0, 0, 0)),
            pl.BlockSpec((1, CONV_W - 1, BRANCH_W), lambda b, t: (b, 0, 0)),
            pl.BlockSpec((1, BRANCH_W, tt), lambda b, t: (b, 0, jnp.maximum(t - t_keep, 0))),
            pl.BlockSpec((1, BRANCH_W, tt), lambda b, t: (b, 0, jnp.maximum(t - t_keep, 0))),
        ],
        out_shape=[
            jax.ShapeDtypeStruct((batch * seq, D_MODEL), BF16),
            jax.ShapeDtypeStruct((batch, RET_HEADS, RET_DK, RET_DV), F32),
            jax.ShapeDtypeStruct((batch, CONV_W - 1, BRANCH_W), F32),
            jax.ShapeDtypeStruct((batch, BRANCH_W, ATT_BAND), F32),
            jax.ShapeDtypeStruct((batch, BRANCH_W, ATT_BAND), F32),
        ],
        scratch_shapes=[
            pltpu.VMEM((2, 2 * RET_DK, 2 * RET_DV), F32),
            pltpu.VMEM((8, BRANCH_W), F32),
            pltpu.VMEM((w, BRANCH_W), BF16),
            pltpu.VMEM((w, BRANCH_W), BF16),
        ],
        compiler_params=_cparams(2),
        name="mixer_prompt",
    )(h, tabs["cos"], tabs["sin"], tabs["dmat"], tabs["qdec"], tabs["kdec"], tabs["blk"], tabs["smask"],
      conv_w, ln_g3, ln_b3, ws, bst, *([bias] * n_blk))


def _mixer_sample_body(h_ref, cos_ref, sin_ref, dmat_ref, qdec_ref, kdec_ref, blk_ref, smask_ref,
                       convw_ref, lng_ref, lnb_ref, ws_ref, bst_ref, bias_ref,
                       sret_in_ref, sconv_in_ref, ck_ref, cv_ref,
                       br_ref, sret_ref, sconv_ref, ko_ref, vo_ref, gv_ref, *, length, group):
    vn = _gmlp(h_ref, lng_ref[0], lnb_ref[0], ws_ref, bst_ref, br_ref, length)
    for g in range(group):
        gv_ref[g] = vn[g * length:(g + 1) * length]

    for g in range(group):
        hg = h_ref.at[g * length:(g + 1) * length]
        brg = br_ref.at[g * length:(g + 1) * length]

        states = _retention(hg, cos_ref[...], sin_ref[...], dmat_ref, qdec_ref[...], kdec_ref[...], blk_ref,
                            smask_ref[...], _state_to_pairs(sret_in_ref.at[0, g]), brg)
        _pairs_to_state(states, sret_ref.at[g])

        prev8 = jnp.concatenate([jnp.zeros((8 - (CONV_W - 1), BRANCH_W), F32), sconv_in_ref[0, g]], axis=0)
        z = _conv(hg, prev8, convw_ref[0], brg)
        sconv_ref[g] = z[length - (CONV_W - 1):length]

        ak = hg[:, C_AK:C_AK + BRANCH_W]
        av = hg[:, C_AV:C_AV + BRANCH_W]
        ko_ref[g] = ak.astype(F32)
        vo_ref[g] = av.astype(F32)
        _attention_sample(hg, ck_ref[0, g].astype(BF16), cv_ref[0, g].astype(BF16), ak, av, bias_ref, brg)


def _mixer_sample(h, tabs, conv_w, ln_g3, ln_b3, ws, bst, bias, state_ret, state_conv, cache_k, cache_v,
                  layer, streams, length, group):
    full = lambda shape: pl.BlockSpec(shape, lambda s: (0,) * len(shape))
    per_layer = lambda shape: pl.BlockSpec((1,) + shape, lambda s: (layer,) + (0,) * len(shape))
    per_stream = lambda shape: pl.BlockSpec((1, group) + shape, lambda s: (layer, s) + (0,) * len(shape))
    out_stream = lambda shape: pl.BlockSpec((group,) + shape, lambda s: (s,) + (0,) * len(shape))
    w = bias.shape[-1]
    return pl.pallas_call(
        functools.partial(_mixer_sample_body, length=length, group=group),
        grid=(streams // group,),
        in_specs=[
            pl.BlockSpec((group * length, MIX_COLS), lambda s: (s, 0)),
            full((length, 256)), full((length, 256)),
            full((RET_HEADS, length, length)),
            full((length, 256)), full((length, 256)),
            full((2, 1, 256)),
            full((2 * RET_DK, 2 * RET_DV)),
            per_layer((CONV_W, BRANCH_W)),
            per_layer((1, BRANCH_W)), per_layer((1, BRANCH_W)),
            per_layer((GMLP_GROUPS, GMLP_BLOCK, GMLP_BLOCK)),
            per_layer((GMLP_BLOCK, GMLP_GROUPS)),
            per_layer((1, ATT_HEADS, length, w)),
            per_stream((RET_HEADS, RET_DK, RET_DV)),
            per_stream((CONV_W - 1, BRANCH_W)),
            per_stream((BRANCH_W, ATT_BAND)),
            per_stream((BRANCH_W, ATT_BAND)),
        ],
        out_specs=[
            pl.BlockSpec((group * length, D_MODEL), lambda s: (s, 0)),
            out_stream((RET_HEADS, RET_DK, RET_DV)),
            out_stream((CONV_W - 1, BRANCH_W)),
            out_stream((length, BRANCH_W)),
            out_stream((length, BRANCH_W)),
            out_stream((length, BRANCH_W)),
        ],
        out_shape=[
            jax.ShapeDtypeStruct((streams * length, D_MODEL), BF16),
            jax.ShapeDtypeStruct((streams, RET_HEADS, RET_DK, RET_DV), F32),
            jax.ShapeDtypeStruct((streams, CONV_W - 1, BRANCH_W), F32),
            jax.ShapeDtypeStruct((streams, length, BRANCH_W), F32),
            jax.ShapeDtypeStruct((streams, length, BRANCH_W), F32),
            jax.ShapeDtypeStruct((streams, length, BRANCH_W), F32),
        ],
        compiler_params=_cparams(1),
        name="mixer_sample",
    )(h, tabs["cos"], tabs["sin"], tabs["dmat"], tabs["qdec"], tabs["kdec"], tabs["blk"], tabs["smask"],
      conv_w, ln_g3, ln_b3, ws, bst, bias, state_ret, state_conv, cache_k, cache_v)


def _mixer_tables(positions, blk_len):
    half = RET_DK // 2
    freqs = ROPE_BASE ** (-jnp.arange(half, dtype=F32) / half)
    ang = jnp.asarray(positions, F32)[:, None] * freqs[None, :]
    cos, sin = jnp.cos(ang), jnp.sin(ang)
    cos_t = jnp.tile(cos, (1, 2 * RET_HEADS))
    sin_t = jnp.tile(jnp.concatenate([-sin, sin], axis=1), (1, RET_HEADS))
    lg = jnp.log1p(-jnp.exp2(-5.0 - jnp.arange(RET_HEADS, dtype=F32)))
    t = jnp.arange(blk_len, dtype=F32)
    diff = t[:, None] - t[None, :]
    dmat = jnp.where(diff >= 0, jnp.exp(jnp.maximum(diff, 0.0)[None] * lg[:, None, None]), 0.0)
    qdec = jnp.repeat(jnp.exp((t[:, None] + 1.0) * lg[None, :]), RET_DK, axis=1)
    kdec = jnp.repeat(jnp.exp((blk_len - 1.0 - t)[:, None] * lg[None, :]), RET_DK, axis=1)
    blk = jnp.repeat(jnp.exp(blk_len * lg), RET_DV).reshape(RET_HEADS // 2, 1, 2 * RET_DV)
    smask = jnp.kron(jnp.eye(2, dtype=F32), jnp.ones((RET_DK, RET_DV), F32))
    return dict(cos=cos_t, sin=sin_t, dmat=dmat, qdec=qdec, kdec=kdec, blk=blk, smask=smask)


def _pick_tile(n, target):
    t = min(n, target)
    while n % t:
        t //= 2
    return t


def kernel(x_prompt, x_sample, state_ret, state_conv, cache_att_k, cache_att_v, norm_gain, w_in, w_branch, w_out,
           conv_w, gmlp_ln_gain, gmlp_ln_bias, gmlp_ws, gmlp_bs, att_rel_bias, final_norm_gain):
    batch, seq, _ = x_prompt.shape
    streams, dec_len, _ = x_sample.shape
    depth = w_in.shape[0]
    assert seq % MIXER_TT == 0 and seq >= ATT_BAND and cache_att_k.shape[2] == ATT_BAND

    tt = MIXER_TT
    n_p, n_s = batch * seq, streams * dec_len
    tm_p, tm_s = _pick_tile(n_p, 1024), _pick_tile(n_s, 512)

    w_mix_t, w_merge_t = _pack_weights(w_in, w_branch, w_out)
    gain3 = norm_gain.reshape(depth, 1, D_MODEL)
    ln_g3 = gmlp_ln_gain.reshape(depth, 1, BRANCH_W)
    ln_b3 = gmlp_ln_bias.reshape(depth, 1, BRANCH_W)
    bst = jnp.swapaxes(gmlp_bs, 1, 2)
    table_pad = jnp.pad(att_rel_bias, ((0, 0), (0, 0), (0, TABLE_LANES - att_rel_bias.shape[-1])))
    to_t = lambda c: jnp.transpose(c, (0, 1, 3, 4, 2)).reshape(depth, streams, BRANCH_W, ATT_BAND)
    cache_k, cache_v = to_t(cache_att_k), to_t(cache_att_v)
    gain2 = final_norm_gain.reshape(1, D_MODEL)

    tabs_p = _mixer_tables(np.arange(seq), tt)
    tabs_s = _mixer_tables(PAST_LEN + np.arange(dec_len), dec_len)

    hp = x_prompt.reshape(n_p, D_MODEL)
    hs = x_sample.reshape(n_s, D_MODEL)
    outs = [[] for _ in range(9)]
    w_p = ATT_BAND + ATT_ROWS
    bias_p = _rel_bias(table_pad, ATT_BAND // ATT_ROWS + 1, ATT_ROWS, w_p, w_p, True)
    w_s = -(-(ATT_BAND + dec_len) // LANES) * LANES
    bias_s = _rel_bias(table_pad, 1, dec_len, w_s, ATT_BAND + dec_len, False)
    for l in range(depth):
        h_mix, xn = _inproj(hp, gain3, w_mix_t, l, tm_p, INPROJ_TN, 1)
        br, p_ret, p_conv, p_k, p_v = _mixer_prompt(h_mix, tabs_p, conv_w, ln_g3, ln_b3, gmlp_ws, bst, bias_p,
                                                    l, batch, seq, tt)
        hp = _merge(xn, br, hp, w_merge_t, gain2, l, tm_p, l == depth - 1, 1)

        h_mix, xn = _inproj(hs, gain3, w_mix_t, l, tm_s, INPROJ_TN, SAMPLE_W_SLOTS)
        br, s_ret, s_conv, s_k, s_v, s_gv = _mixer_sample(h_mix, tabs_s, conv_w, ln_g3, ln_b3, gmlp_ws, bst, bias_s,
                                                          state_ret, state_conv, cache_k, cache_v,
                                                          l, streams, dec_len, _pick_tile(streams, SAMPLE_GROUP))
        hs = _merge(xn, br, hs, w_merge_t, gain2, l, tm_s, l == depth - 1, SAMPLE_W_SLOTS)

        for dst, val in zip(outs, (p_ret, p_conv, p_k, p_v, s_ret, s_conv, s_k, s_v, s_gv)):
            dst.append(val)

    y_prompt = hp.reshape(batch, seq, D_MODEL)
    y_sample = hs.reshape(streams, dec_len, D_MODEL)
    p_ret, p_conv, p_k, p_v, s_ret, s_conv, s_k, s_v, s_gv = [jnp.stack(o) for o in outs]
    kv_shape = lambda a: a.reshape(a.shape[:3] + (ATT_HEADS, ATT_DH))
    kv_from_t = lambda a: jnp.transpose(a.reshape(a.shape[:2] + (ATT_HEADS, ATT_DH, ATT_BAND)), (0, 1, 4, 2, 3))
    return (y_prompt, y_sample, p_ret, p_conv, kv_from_t(p_k), kv_from_t(p_v),
            s_ret, s_conv, kv_shape(s_k), kv_shape(s_v), s_gv)
```

```python
import functools

import numpy as np
import jax
import jax.numpy as jnp
from jax import lax
from jax.experimental import pallas as pl
from jax.experimental.pallas import tpu as pltpu

F32 = jnp.float32
BF16 = jnp.bfloat16

D_MODEL = 2048
BRANCH_W = 512
N_BRANCH = 4
CHUNK = 64
RET_HEADS = 4
RET_DK = 64
RET_DV = 128
ROPE_BASE = 10000.0
CONV_W = 3
GMLP_BLOCK = 128
GMLP_GROUPS = 4
ATT_HEADS = 8
ATT_DH = 64
ATT_BAND = 512
MAX_REL = 128
NORM_EPS = 1e-6
PAST_LEN = 2048
NEG_BIG = -1e30

MIX_COLS = 2 * RET_HEADS * RET_DK + 13 * BRANCH_W
IN_COLS = MIX_COLS + N_BRANCH * D_MODEL
C_RQ, C_RK, C_RV, C_RG = 0, 256, 512, 1024
C_CB, C_CC, C_CX, C_CG = 1536, 2048, 2560, 3072
C_MU, C_MV, C_MG = 3584, 4096, 4608
C_AQ, C_AK, C_AV, C_AG = 5120, 5632, 6144, 6656

LANES = 128
WBLK = 256
INPROJ_TN = 7 * WBLK
ATT_ROWS = 256
MIXER_TT = 256
MERGE_ROW_CHUNK = 256
NORM_ROW_CHUNK = 256
SAMPLE_GROUP = 4
LOG2E = 1.4426950408889634
VMEM_LIMIT_BYTES = 56 * 1024 * 1024
TABLE_LANES = 384


def _cparams(n_axes):
    return pltpu.CompilerParams(dimension_semantics=("arbitrary",) * n_axes,
                                vmem_limit_bytes=VMEM_LIMIT_BYTES)


def _dot(a, b):
    return jnp.dot(a, b, preferred_element_type=F32)


def _dot_nt(a, b):
    return lax.dot_general(a, b, (((1,), (1,)), ((), ())), preferred_element_type=F32)


def _dot_tn(a, b):
    return lax.dot_general(a, b, (((0,), (0,)), ((), ())), preferred_element_type=F32)


def _silu(x):
    return x * jax.nn.sigmoid(x)


def _inproj_body(x_ref, g_ref, w_ref, h_ref, xn_ref, *, nsub):
    j = pl.program_id(1)
    tm = x_ref.shape[0]
    rc = min(tm, NORM_ROW_CHUNK)

    @pl.when(j == 0)
    def _():
        for r0 in range(0, tm, rc):
            x = x_ref[r0:r0 + rc, :]
            inv = lax.rsqrt(jnp.mean(x * x, axis=-1, keepdims=True) + NORM_EPS)
            xn = (x * inv * g_ref[0]).astype(BF16)
            xn_ref[r0:r0 + rc, :] = xn
            for c in range(nsub):
                h_ref[r0:r0 + rc, c * WBLK:(c + 1) * WBLK] = _dot(xn, w_ref[0, c]).astype(h_ref.dtype)

    @pl.when(j > 0)
    def _():
        xn = xn_ref[...]
        for c in range(nsub):
            h_ref[:, c * WBLK:(c + 1) * WBLK] = _dot(xn, w_ref[0, c]).astype(h_ref.dtype)


def _inproj(x2d, gain3, w_mix_t, layer, tm, tn):
    n = x2d.shape[0]
    nsub = tn // WBLK
    return pl.pallas_call(
        functools.partial(_inproj_body, nsub=nsub),
        grid=(n // tm, MIX_COLS // tn),
        in_specs=[
            pl.BlockSpec((tm, D_MODEL), lambda i, j: (i, 0)),
            pl.BlockSpec((1, 1, D_MODEL), lambda i, j: (layer, 0, 0)),
            pl.BlockSpec((1, nsub, D_MODEL, WBLK), lambda i, j: (layer, j, 0, 0)),
        ],
        out_specs=[
            pl.BlockSpec((tm, tn), lambda i, j: (i, j)),
            pl.BlockSpec((tm, D_MODEL), lambda i, j: (i, 0)),
        ],
        out_shape=[
            jax.ShapeDtypeStruct((n, MIX_COLS), BF16),
            jax.ShapeDtypeStruct((n, D_MODEL), BF16),
        ],
        compiler_params=_cparams(2),
        name="inproj",
    )(x2d, gain3, w_mix_t)


MERGE_ROWS = N_BRANCH * D_MODEL + N_BRANCH * BRANCH_W + D_MODEL
MERGE_WO_ROW0 = N_BRANCH * D_MODEL + N_BRANCH * BRANCH_W


def _add_out_projection(hm_sc, wo_ref, out_ref, first):
    hm = hm_sc[...]
    for j in range(D_MODEL // WBLK):
        upd = _dot(hm, wo_ref[j * WBLK:(j + 1) * WBLK, :])
        if first and j > 0:
            out_ref[:, j * WBLK:(j + 1) * WBLK] = upd
        else:
            out_ref[:, j * WBLK:(j + 1) * WBLK] += upd


def _merge_body(xn_ref, br_ref, x_ref, w_ref, *rest, final_norm):
    fg_ref, out_ref, hm_sc, wlast_sc = rest if final_norm else (None,) + rest
    s = pl.program_id(1)
    nb = D_MODEL // WBLK
    tm = out_ref.shape[0]
    wo_ref = w_ref.at[0, 0, MERGE_WO_ROW0:MERGE_ROWS]

    @pl.when(s == 0)
    def _():
        out_ref[:, 0:WBLK] = x_ref[...]
        wlast_sc[...] = wo_ref[...]

    @pl.when(s == 1)
    def _():
        _add_out_projection(hm_sc, wo_ref, out_ref, True)

    @pl.when(s > 1)
    def _():
        _add_out_projection(hm_sc, wo_ref, out_ref, False)

    rc = min(tm, MERGE_ROW_CHUNK)
    for r in range(tm // rc):
        rows = slice(r * rc, (r + 1) * rc)
        xn = xn_ref[rows, :]
        acc = None
        for i in range(N_BRANCH):
            logits = _dot(xn, w_ref[0, 0, i * D_MODEL:(i + 1) * D_MODEL, :])
            wb0 = N_BRANCH * D_MODEL + i * BRANCH_W
            proj = _dot(br_ref[rows, i * BRANCH_W:(i + 1) * BRANCH_W], w_ref[0, 0, wb0:wb0 + BRANCH_W, :])
            term = jax.nn.sigmoid(logits) * proj
            acc = term if acc is None else acc + term
        hm_sc[rows, :] = acc.astype(BF16)

    @pl.when(s == nb - 1)
    def _():
        _add_out_projection(hm_sc, wlast_sc, out_ref, False)

    for c in range(1, nb):
        @pl.when(s == c)
        def _():
            out_ref[:, c * WBLK:(c + 1) * WBLK] += x_ref[...]

    if final_norm:
        @pl.when(s == nb - 1)
        def _():
            y = out_ref[...]
            r = lax.rsqrt(jnp.mean(y * y, axis=-1, keepdims=True) + NORM_EPS)
            out_ref[...] = y * r * fg_ref[...]


def _merge(xn, br, x2d, w_merge_t, final_gain2, layer, tm, final_norm):
    n = x2d.shape[0]
    nb = D_MODEL // WBLK
    in_specs = [
        pl.BlockSpec((tm, D_MODEL), lambda m, s: (m, 0)),
        pl.BlockSpec((tm, D_MODEL), lambda m, s: (m, 0)),
        pl.BlockSpec((tm, WBLK), lambda m, s: (m, s)),
        pl.BlockSpec((1, 1, MERGE_ROWS, WBLK), lambda m, s: (layer, s, 0, 0)),
    ]
    operands = [xn, br, x2d, w_merge_t]
    if final_norm:
        in_specs.append(pl.BlockSpec((1, D_MODEL), lambda m, s: (0, 0)))
        operands.append(final_gain2)
    return pl.pallas_call(
        functools.partial(_merge_body, final_norm=final_norm),
        grid=(n // tm, nb),
        in_specs=in_specs,
        out_specs=pl.BlockSpec((tm, D_MODEL), lambda m, s: (m, 0)),
        out_shape=jax.ShapeDtypeStruct((n, D_MODEL), F32),
        scratch_shapes=[pltpu.VMEM((tm, WBLK), BF16), pltpu.VMEM((D_MODEL, WBLK), BF16)],
        compiler_params=_cparams(2),
        name="merge",
    )(*operands)


def _pack_mix_body(w_ref, o_ref, *, nsub):
    for c in range(nsub):
        o_ref[0, c] = w_ref[0, :, c * WBLK:(c + 1) * WBLK].astype(BF16)


def _pack_merge_body(g0_ref, g1_ref, g2_ref, g3_ref, wb_ref, wo_ref, o_ref):
    for i, g_ref in enumerate((g0_ref, g1_ref, g2_ref, g3_ref)):
        o_ref[0, 0, i * D_MODEL:(i + 1) * D_MODEL, :] = g_ref[0].astype(BF16)
    r0 = N_BRANCH * D_MODEL
    o_ref[0, 0, r0:r0 + N_BRANCH * BRANCH_W, :] = wb_ref[0].astype(BF16)
    r0 += N_BRANCH * BRANCH_W
    for j in range(D_MODEL // WBLK):
        o_ref[0, 0, r0 + j * WBLK:r0 + (j + 1) * WBLK, :] = wo_ref[0, :, j * WBLK:(j + 1) * WBLK].astype(BF16)


def _pack_weights(w_in, w_branch, w_out):
    depth = w_in.shape[0]
    nb = D_MODEL // WBLK
    nsub = INPROJ_TN // WBLK
    w_mix_t = pl.pallas_call(
        functools.partial(_pack_mix_body, nsub=nsub),
        grid=(depth, MIX_COLS // INPROJ_TN),
        in_specs=[pl.BlockSpec((1, D_MODEL, INPROJ_TN), lambda l, j: (l, 0, j))],
        out_specs=pl.BlockSpec((1, nsub, D_MODEL, WBLK), lambda l, j: (l, j, 0, 0)),
        out_shape=jax.ShapeDtypeStruct((depth, MIX_COLS // WBLK, D_MODEL, WBLK), BF16),
        compiler_params=_cparams(2),
        name="pack_mix",
    )(w_in)

    gate_blk0 = MIX_COLS // WBLK
    gate_spec = lambda i: pl.BlockSpec((1, D_MODEL, WBLK), lambda l, c: (l, 0, gate_blk0 + i * nb + c))
    w_merge_t = pl.pallas_call(
        _pack_merge_body,
        grid=(depth, nb),
        in_specs=[
            gate_spec(0), gate_spec(1), gate_spec(2), gate_spec(3),
            pl.BlockSpec((1, N_BRANCH * BRANCH_W, WBLK), lambda l, c: (l, 0, c)),
            pl.BlockSpec((1, WBLK, D_MODEL), lambda l, c: (l, (c + nb - 1) % nb, 0)),
        ],
        out_specs=pl.BlockSpec((1, 1, MERGE_ROWS, WBLK), lambda l, c: (l, c, 0, 0)),
        out_shape=jax.ShapeDtypeStruct((depth, nb, MERGE_ROWS, WBLK), BF16),
        compiler_params=_cparams(2),
        name="pack_merge",
    )(w_in, w_in, w_in, w_in, w_branch.reshape(depth, N_BRANCH * BRANCH_W, D_MODEL), w_out)
    return w_mix_t, w_merge_t


def _bias_body(tab_ref, perm_ref, out_ref, *, n_var, tq, w, w_valid, wpad, banded):
    t = tab_ref[0] * LOG2E
    perm = perm_ref[...]
    t_hi = t.astype(BF16)
    r1 = t - t_hi.astype(F32)
    t_mid = r1.astype(BF16)
    t_lo = (r1 - t_mid.astype(F32)).astype(BF16)
    row0 = _dot(t_hi, perm) + _dot(t_mid, perm) + _dot(t_lo, perm)
    qi = lax.broadcasted_iota(jnp.int32, (tq, w), 0)
    kj = lax.broadcasted_iota(jnp.int32, (tq, w), 1)
    visible = kj < w_valid
    if banded:
        lo = (qi // CHUNK) * CHUNK
        visible = visible & (kj >= lo) & (kj < lo + ATT_BAND + CHUNK)
    for h in range(ATT_HEADS):
        full = jnp.broadcast_to(row0[h:h + 1, :], (tq, wpad))
        shifted = pltpu.roll(full, 0, 1, stride=1, stride_axis=0)[:, :w]
        for u in range(n_var):
            vis_u = visible & (kj >= ATT_BAND - u * tq) if banded else visible
            out_ref[0, u, h] = jnp.where(vis_u, shifted, NEG_BIG)


def _bias_perm(w, wpad):
    j = np.arange(wpad)
    m = np.clip(ATT_BAND - j, -MAX_REL, MAX_REL) + MAX_REL
    m = np.where(j >= w, 2 * MAX_REL, m)
    perm = np.zeros((TABLE_LANES, wpad), np.float32)
    perm[m, j] = 1.0
    return jnp.asarray(perm, BF16)


def _rel_bias(table_pad, n_var, tq, w, w_valid, banded):
    depth = table_pad.shape[0]
    wpad = -(-(w + tq) // LANES) * LANES
    return pl.pallas_call(
        functools.partial(_bias_body, n_var=n_var, tq=tq, w=w, w_valid=w_valid, wpad=wpad, banded=banded),
        grid=(depth,),
        in_specs=[
            pl.BlockSpec((1, ATT_HEADS, TABLE_LANES), lambda l: (l, 0, 0)),
            pl.BlockSpec((TABLE_LANES, wpad), lambda l: (0, 0)),
        ],
        out_specs=pl.BlockSpec((1, n_var, ATT_HEADS, tq, w), lambda l: (l, 0, 0, 0, 0)),
        out_shape=jax.ShapeDtypeStruct((depth, n_var, ATT_HEADS, tq, w), F32),
        compiler_params=_cparams(1),
        name="rel_bias",
    )(table_pad, _bias_perm(w, wpad))


def _rotary(x, cos, sin_signed):
    lane = lax.broadcasted_iota(jnp.int32, x.shape, 1)
    first_half = (lane % RET_DK) < (RET_DK // 2)
    width = x.shape[1]
    swapped = jnp.where(first_half, pltpu.roll(x, width - RET_DK // 2, 1), pltpu.roll(x, RET_DK // 2, 1))
    return x * cos + swapped * sin_signed


def _retention(h_ref, cos, sin_signed, dmat_ref, qdec, kdec, blk_ref, smask, states, br_ref):
    length = h_ref.shape[0]
    q = _rotary(h_ref[:, C_RQ:C_RQ + 256].astype(F32), cos, sin_signed)
    k = _rotary(h_ref[:, C_RK:C_RK + 256].astype(F32), cos, sin_signed) * (RET_DK ** -0.5)
    kb = k.astype(BF16)
    qd, kd = (q * qdec).astype(BF16), (k * kdec).astype(BF16)
    first = lax.broadcasted_iota(jnp.int32, (length, LANES), 1) < RET_DK
    new_states = []
    for p in range(RET_HEADS // 2):
        cols = slice(p * LANES, (p + 1) * LANES)
        sp = states[p]
        v_pair = h_ref[:, C_RV + p * 2 * RET_DV:C_RV + (p + 1) * 2 * RET_DV]
        o_inter = _dot(qd[:, cols], sp.astype(BF16))
        q2 = jnp.concatenate([jnp.where(first, q[:, cols], 0.0), jnp.where(first, 0.0, q[:, cols])],
                             axis=0).astype(BF16)
        decay2 = dmat_ref[2 * p:2 * p + 2].reshape(2 * length, length)
        inner = (_dot_nt(q2, kb[:, cols]) * decay2).astype(BF16)
        o2 = _dot(inner, v_pair)
        for half in range(2):
            hh = 2 * p + half
            vl = slice(half * RET_DV, (half + 1) * RET_DV)
            o = o2[half * length:(half + 1) * length, vl] + o_inter[:, vl]
            mu = jnp.mean(o, axis=-1, keepdims=True)
            oc = o - mu
            var = jnp.mean(oc * oc, axis=-1, keepdims=True)
            hn = oc * lax.rsqrt(var + NORM_EPS)
            gcols = slice(C_RG + hh * RET_DV, C_RG + (hh + 1) * RET_DV)
            br_ref[:, hh * RET_DV:(hh + 1) * RET_DV] = (_silu(h_ref[:, gcols].astype(F32)) * hn).astype(BF16)
        upd = _dot_tn(kd[:, cols], v_pair)
        new_states.append((sp * blk_ref[p] + upd) * smask)
    return new_states


def _conv(h_ref, prev8, convw, br_ref):
    z = h_ref[:, C_CC:C_CC + BRANCH_W].astype(F32) * h_ref[:, C_CX:C_CX + BRANCH_W].astype(F32)
    zcat = jnp.concatenate([prev8, z], axis=0)
    z1 = pltpu.roll(zcat, 1, 0)[8:]
    z2 = pltpu.roll(zcat, 2, 0)[8:]
    y = convw[0:1] * z2 + convw[1:2] * z1 + convw[2:3] * z
    cb = h_ref[:, C_CB:C_CB + BRANCH_W].astype(F32)
    cg = h_ref[:, C_CG:C_CG + BRANCH_W].astype(F32)
    br_ref[:, BRANCH_W:2 * BRANCH_W] = (_silu(cg) * (cb * y)).astype(BF16)
    return z


def _gmlp(h_ref, lng, lnb, ws_ref, bst_ref, br_ref, blk_len):
    length = h_ref.shape[0]
    mv = h_ref[:, C_MV:C_MV + BRANCH_W].astype(F32)
    mean = jnp.mean(mv, axis=-1, keepdims=True)
    cen = mv - mean
    var = jnp.mean(cen * cen, axis=-1, keepdims=True)
    vn = cen * lax.rsqrt(var + NORM_EPS) * lng + lnb
    vnb = vn.astype(BF16)
    ri = lax.broadcasted_iota(jnp.int32, (blk_len, blk_len), 0)
    ci = lax.broadcasted_iota(jnp.int32, (blk_len, blk_len), 1)
    causal = (ci // CHUNK) <= (ri // CHUNK)
    n_blocks = length // blk_len
    for g in range(GMLP_GROUPS):
        wg = jnp.where(causal, ws_ref[0, g, 0:blk_len, 0:blk_len], 0.0).astype(BF16)
        bcol = bst_ref[0, 0:blk_len, g:g + 1]
        gc = slice(g * LANES, (g + 1) * LANES)
        side = jnp.concatenate([vnb[n * blk_len:(n + 1) * blk_len, gc] for n in range(n_blocks)], axis=1)
        mix_all = _dot(wg, side) + bcol
        for n in range(n_blocks):
            rows = slice(n * blk_len, (n + 1) * blk_len)
            mix = mix_all[:, n * LANES:(n + 1) * LANES]
            mu = h_ref[rows, C_MU + g * LANES:C_MU + (g + 1) * LANES].astype(F32)
            mg = h_ref[rows, C_MG + g * LANES:C_MG + (g + 1) * LANES].astype(F32)
            br_ref[rows, 2 * BRANCH_W + g * LANES:2 * BRANCH_W + (g + 1) * LANES] = (
                _silu(mg) * (mu * mix)).astype(BF16)
    return vn


def _attention(h_ref, kh_ref, vh_ref, bias_ref, br_ref, r0, rows, c0, width):
    lane = lax.broadcasted_iota(jnp.int32, (rows, LANES), 1)
    rsl = slice(r0, r0 + rows)
    for p in range(ATT_HEADS // 2):
        cols = slice(p * LANES, (p + 1) * LANES)
        qp = h_ref[rsl, C_AQ + p * LANES:C_AQ + (p + 1) * LANES].astype(F32) * (LOG2E * ATT_DH ** -0.5)
        kp = kh_ref[c0:c0 + width, cols]
        vp = vh_ref[c0:c0 + width, cols]
        first = lane < ATT_DH
        q2 = jnp.concatenate([jnp.where(first, qp, 0.0), jnp.where(first, 0.0, qp)], axis=0).astype(BF16)
        s = _dot_nt(q2, kp) + bias_ref[0, 2 * p:2 * p + 2].reshape(2 * rows, width)
        m = jnp.max(s, axis=-1, keepdims=True)
        e = jnp.exp2(s - m)
        denom = jnp.sum(e, axis=-1, keepdims=True)
        o = _dot(e.astype(BF16), vp) / denom
        acc = jnp.where(first, o[0:rows], o[rows:2 * rows])
        ag = h_ref[rsl, C_AG + p * LANES:C_AG + (p + 1) * LANES].astype(F32)
        br_ref[rsl, 3 * BRANCH_W + p * LANES:3 * BRANCH_W + (p + 1) * LANES] = (_silu(ag) * acc).astype(BF16)


def _attention_sample(h_ref, kt, vt, ak, av, bias_ref, br_ref):
    length = h_ref.shape[0]
    n_old = kt.shape[1]
    lane_head = lax.broadcasted_iota(jnp.int32, (length, BRANCH_W), 1) // ATT_DH
    q = h_ref[:, C_AQ:C_AQ + BRANCH_W].astype(F32) * (LOG2E * ATT_DH ** -0.5)
    q_st = jnp.concatenate([jnp.where(lane_head == hh, q, 0.0) for hh in range(ATT_HEADS)], axis=0).astype(BF16)
    bias = bias_ref[0, 0].reshape(ATT_HEADS * length, bias_ref.shape[-1])
    s_old = _dot(q_st, kt) + bias[:, 0:n_old]
    s_new = _dot_nt(q_st, ak) + bias[:, n_old:n_old + length]
    m = jnp.maximum(jnp.max(s_old, axis=-1, keepdims=True), jnp.max(s_new, axis=-1, keepdims=True))
    e_old = jnp.exp2(s_old - m)
    e_new = jnp.exp2(s_new - m)
    denom = jnp.sum(e_old, axis=-1, keepdims=True) + jnp.sum(e_new, axis=-1, keepdims=True)
    o_st = (_dot_nt(e_old.astype(BF16), vt) + _dot(e_new.astype(BF16), av)) / denom
    o = None
    for hh in range(ATT_HEADS):
        part = jnp.where(lane_head == hh, o_st[hh * length:(hh + 1) * length], 0.0)
        o = part if o is None else o + part
    ag = h_ref[:, C_AG:C_AG + BRANCH_W].astype(F32)
    br_ref[:, 3 * BRANCH_W:4 * BRANCH_W] = (_silu(ag) * o).astype(BF16)


def _state_to_pairs(state_ref):
    pairs = []
    zero = jnp.zeros((RET_DK, RET_DV), F32)
    for p in range(RET_HEADS // 2):
        top = jnp.concatenate([state_ref[2 * p], zero], axis=1)
        bot = jnp.concatenate([zero, state_ref[2 * p + 1]], axis=1)
        pairs.append(jnp.concatenate([top, bot], axis=0))
    return pairs


def _pairs_to_state(pairs, out_ref):
    for p in range(RET_HEADS // 2):
        out_ref[2 * p] = pairs[p][0:RET_DK, 0:RET_DV]
        out_ref[2 * p + 1] = pairs[p][RET_DK:2 * RET_DK, RET_DV:2 * RET_DV]


def _mixer_prompt_body(h_ref, cos_ref, sin_ref, dmat_ref, qdec_ref, kdec_ref, blk_ref, smask_ref,
                       convw_ref, lng_ref, lnb_ref, ws_ref, bst_ref, *rest, tt):
    n_blk = tt // ATT_ROWS
    bias_refs = rest[:n_blk]
    br_ref, sret_ref, sconv_ref, sp_sc, zc_sc, kh_sc, vh_sc = rest[n_blk:]
    t = pl.program_id(1)

    @pl.when(t == 0)
    def _():
        sp_sc[...] = jnp.zeros_like(sp_sc)
        zc_sc[...] = jnp.zeros_like(zc_sc)
        kh_sc[0:ATT_BAND] = jnp.zeros((ATT_BAND, BRANCH_W), BF16)
        vh_sc[0:ATT_BAND] = jnp.zeros((ATT_BAND, BRANCH_W), BF16)

    states = _retention(h_ref, cos_ref[...], sin_ref[...], dmat_ref, qdec_ref[...], kdec_ref[...], blk_ref,
                        smask_ref[...], [sp_sc[0], sp_sc[1]], br_ref)
    sp_sc[0] = states[0]
    sp_sc[1] = states[1]
    _pairs_to_state(states, sret_ref.at[0])

    z = _conv(h_ref, zc_sc[...], convw_ref[0], br_ref)
    tail = z[tt - (CONV_W - 1):tt]
    zc_sc[8 - (CONV_W - 1):8] = tail
    sconv_ref[0] = tail

    _gmlp(h_ref, lng_ref[0], lnb_ref[0], ws_ref, bst_ref, br_ref, GMLP_BLOCK)

    ak = h_ref[:, C_AK:C_AK + BRANCH_W]
    av = h_ref[:, C_AV:C_AV + BRANCH_W]
    kh_sc[ATT_BAND:ATT_BAND + tt] = ak
    vh_sc[ATT_BAND:ATT_BAND + tt] = av
    for blk, bias_ref in enumerate(bias_refs):
        _attention(h_ref, kh_sc, vh_sc, bias_ref.at[0], br_ref, blk * ATT_ROWS, ATT_ROWS, blk * ATT_ROWS,
                   ATT_BAND + ATT_ROWS)
    k_keep = kh_sc[tt:tt + ATT_BAND]
    v_keep = vh_sc[tt:tt + ATT_BAND]
    kh_sc[0:ATT_BAND] = k_keep
    vh_sc[0:ATT_BAND] = v_keep


def _mixer_prompt(h, tabs, conv_w, ln_g3, ln_b3, ws, bst, bias, layer, batch, seq, tt):
    nt = seq // tt
    full = lambda shape: pl.BlockSpec(shape, lambda b, t: (0,) * len(shape))
    per_layer = lambda shape: pl.BlockSpec((1,) + shape, lambda b, t: (layer,) + (0,) * len(shape))
    w = ATT_BAND + tt
    n_blk = tt // ATT_ROWS
    n_var = bias.shape[1]
    bias_spec = lambda blk: pl.BlockSpec((1, 1, ATT_HEADS, ATT_ROWS, ATT_BAND + ATT_ROWS),
                                         lambda b, t: (layer, jnp.minimum(n_blk * t + blk, n_var - 1), 0, 0, 0))
    return pl.pallas_call(
        functools.partial(_mixer_prompt_body, tt=tt),
        grid=(batch, nt),
        in_specs=[
            pl.BlockSpec((tt, MIX_COLS), lambda b, t: (b * nt + t, 0)),
            pl.BlockSpec((tt, 256), lambda b, t: (t, 0)),
            pl.BlockSpec((tt, 256), lambda b, t: (t, 0)),
            full((RET_HEADS, tt, tt)),
            full((tt, 256)), full((tt, 256)),
            full((2, 1, 256)),
            full((2 * RET_DK, 2 * RET_DV)),
            per_layer((CONV_W, BRANCH_W)),
            per_layer((1, BRANCH_W)), per_layer((1, BRANCH_W)),
            per_layer((GMLP_GROUPS, GMLP_BLOCK, GMLP_BLOCK)),
            per_layer((GMLP_BLOCK, GMLP_GROUPS)),
        ] + [bias_spec(blk) for blk in range(n_blk)],
        out_specs=[
            pl.BlockSpec((tt, D_MODEL), lambda b, t: (b * nt + t, 0)),
            pl.BlockSpec((1, RET_HEADS, RET_DK, RET_DV), lambda b, t: (b, 0, 0, 0)),
            pl.BlockSpec((1, CONV_W - 1, BRANCH_W), lambda b, t: (b, 0, 0)),
        ],
        out_shape=[
            jax.ShapeDtypeStruct((batch * seq, D_MODEL), BF16),
            jax.ShapeDtypeStruct((batch, RET_HEADS, RET_DK, RET_DV), F32),
            jax.ShapeDtypeStruct((batch, CONV_W - 1, BRANCH_W), F32),
        ],
        scratch_shapes=[
            pltpu.VMEM((2, 2 * RET_DK, 2 * RET_DV), F32),
            pltpu.VMEM((8, BRANCH_W), F32),
            pltpu.VMEM((w, BRANCH_W), BF16),
            pltpu.VMEM((w, BRANCH_W), BF16),
        ],
        compiler_params=_cparams(2),
        name="mixer_prompt",
    )(h, tabs["cos"], tabs["sin"], tabs["dmat"], tabs["qdec"], tabs["kdec"], tabs["blk"], tabs["smask"],
      conv_w, ln_g3, ln_b3, ws, bst, *([bias] * n_blk))


def _kv_out_body(*refs, depth):
    k_refs, v_refs, (ko_ref, vo_ref) = refs[:depth], refs[depth:2 * depth], refs[2 * depth:]
    for l in range(depth):
        ko_ref[l, 0] = k_refs[l][...].astype(F32).T
        vo_ref[l, 0] = v_refs[l][...].astype(F32).T


def _kv_out(h_list, batch, seq):
    depth = len(h_list)
    rows = MIXER_TT
    first = (seq - ATT_BAND) // rows
    col = lambda c0: pl.BlockSpec((rows, BRANCH_W), lambda b, r: (b * (seq // rows) + first + r, c0 // BRANCH_W))
    out_spec = pl.BlockSpec((depth, 1, BRANCH_W, rows), lambda b, r: (0, b, 0, r))
    out_shape = jax.ShapeDtypeStruct((depth, batch, BRANCH_W, ATT_BAND), F32)
    return pl.pallas_call(
        functools.partial(_kv_out_body, depth=depth),
        grid=(batch, ATT_BAND // rows),
        in_specs=[col(C_AK)] * depth + [col(C_AV)] * depth,
        out_specs=[out_spec, out_spec],
        out_shape=[out_shape, out_shape],
        compiler_params=_cparams(2),
        name="kv_out",
    )(*h_list, *h_list)


def _mixer_sample_body(h_ref, cos_ref, sin_ref, dmat_ref, qdec_ref, kdec_ref, blk_ref, smask_ref,
                       convw_ref, lng_ref, lnb_ref, ws_ref, bst_ref, bias_ref,
                       sret_in_ref, sconv_in_ref, ck_ref, cv_ref,
                       br_ref, sret_ref, sconv_ref, ko_ref, vo_ref, gv_ref, *, length, group):
    vn = _gmlp(h_ref, lng_ref[0], lnb_ref[0], ws_ref, bst_ref, br_ref, length)
    for g in range(group):
        gv_ref[g] = vn[g * length:(g + 1) * length]

    for g in range(group):
        hg = h_ref.at[g * length:(g + 1) * length]
        brg = br_ref.at[g * length:(g + 1) * length]

        states = _retention(hg, cos_ref[...], sin_ref[...], dmat_ref, qdec_ref[...], kdec_ref[...], blk_ref,
                            smask_ref[...], _state_to_pairs(sret_in_ref.at[0, g]), brg)
        _pairs_to_state(states, sret_ref.at[g])

        prev8 = jnp.concatenate([jnp.zeros((8 - (CONV_W - 1), BRANCH_W), F32), sconv_in_ref[0, g]], axis=0)
        z = _conv(hg, prev8, convw_ref[0], brg)
        sconv_ref[g] = z[length - (CONV_W - 1):length]

        ak = hg[:, C_AK:C_AK + BRANCH_W]
        av = hg[:, C_AV:C_AV + BRANCH_W]
        ko_ref[g] = ak.astype(F32)
        vo_ref[g] = av.astype(F32)
        _attention_sample(hg, ck_ref[0, g].astype(BF16), cv_ref[0, g].astype(BF16), ak, av, bias_ref, brg)


def _mixer_sample(h, tabs, conv_w, ln_g3, ln_b3, ws, bst, bias, state_ret, state_conv, cache_k, cache_v,
                  layer, streams, length, group):
    full = lambda shape: pl.BlockSpec(shape, lambda s: (0,) * len(shape))
    per_layer = lambda shape: pl.BlockSpec((1,) + shape, lambda s: (layer,) + (0,) * len(shape))
    per_stream = lambda shape: pl.BlockSpec((1, group) + shape, lambda s: (layer, s) + (0,) * len(shape))
    out_stream = lambda shape: pl.BlockSpec((group,) + shape, lambda s: (s,) + (0,) * len(shape))
    w = bias.shape[-1]
    return pl.pallas_call(
        functools.partial(_mixer_sample_body, length=length, group=group),
        grid=(streams // group,),
        in_specs=[
            pl.BlockSpec((group * length, MIX_COLS), lambda s: (s, 0)),
            full((length, 256)), full((length, 256)),
            full((RET_HEADS, length, length)),
            full((length, 256)), full((length, 256)),
            full((2, 1, 256)),
            full((2 * RET_DK, 2 * RET_DV)),
            per_layer((CONV_W, BRANCH_W)),
            per_layer((1, BRANCH_W)), per_layer((1, BRANCH_W)),
            per_layer((GMLP_GROUPS, GMLP_BLOCK, GMLP_BLOCK)),
            per_layer((GMLP_BLOCK, GMLP_GROUPS)),
            per_layer((1, ATT_HEADS, length, w)),
            per_stream((RET_HEADS, RET_DK, RET_DV)),
            per_stream((CONV_W - 1, BRANCH_W)),
            per_stream((BRANCH_W, ATT_BAND)),
            per_stream((BRANCH_W, ATT_BAND)),
        ],
        out_specs=[
            pl.BlockSpec((group * length, D_MODEL), lambda s: (s, 0)),
            out_stream((RET_HEADS, RET_DK, RET_DV)),
            out_stream((CONV_W - 1, BRANCH_W)),
            out_stream((length, BRANCH_W)),
            out_stream((length, BRANCH_W)),
            out_stream((length, BRANCH_W)),
        ],
        out_shape=[
            jax.ShapeDtypeStruct((streams * length, D_MODEL), BF16),
            jax.ShapeDtypeStruct((streams, RET_HEADS, RET_DK, RET_DV), F32),
            jax.ShapeDtypeStruct((streams, CONV_W - 1, BRANCH_W), F32),
            jax.ShapeDtypeStruct((streams, length, BRANCH_W), F32),
            jax.ShapeDtypeStruct((streams, length, BRANCH_W), F32),
            jax.ShapeDtypeStruct((streams, length, BRANCH_W), F32),
        ],
        compiler_params=_cparams(1),
        name="mixer_sample",
    )(h, tabs["cos"], tabs["sin"], tabs["dmat"], tabs["qdec"], tabs["kdec"], tabs["blk"], tabs["smask"],
      conv_w, ln_g3, ln_b3, ws, bst, bias, state_ret, state_conv, cache_k, cache_v)


def _mixer_tables(positions, blk_len):
    half = RET_DK // 2
    freqs = ROPE_BASE ** (-jnp.arange(half, dtype=F32) / half)
    ang = jnp.asarray(positions, F32)[:, None] * freqs[None, :]
    cos, sin = jnp.cos(ang), jnp.sin(ang)
    cos_t = jnp.tile(cos, (1, 2 * RET_HEADS))
    sin_t = jnp.tile(jnp.concatenate([-sin, sin], axis=1), (1, RET_HEADS))
    lg = jnp.log1p(-jnp.exp2(-5.0 - jnp.arange(RET_HEADS, dtype=F32)))
    t = jnp.arange(blk_len, dtype=F32)
    diff = t[:, None] - t[None, :]
    dmat = jnp.where(diff >= 0, jnp.exp(jnp.maximum(diff, 0.0)[None] * lg[:, None, None]), 0.0)
    qdec = jnp.repeat(jnp.exp((t[:, None] + 1.0) * lg[None, :]), RET_DK, axis=1)
    kdec = jnp.repeat(jnp.exp((blk_len - 1.0 - t)[:, None] * lg[None, :]), RET_DK, axis=1)
    blk = jnp.repeat(jnp.exp(blk_len * lg), RET_DV).reshape(RET_HEADS // 2, 1, 2 * RET_DV)
    smask = jnp.kron(jnp.eye(2, dtype=F32), jnp.ones((RET_DK, RET_DV), F32))
    return dict(cos=cos_t, sin=sin_t, dmat=dmat, qdec=qdec, kdec=kdec, blk=blk, smask=smask)


def _pick_tile(n, target):
    t = min(n, target)
    while n % t:
        t //= 2
    return t


def kernel(x_prompt, x_sample, state_ret, state_conv, cache_att_k, cache_att_v, norm_gain, w_in, w_branch, w_out,
           conv_w, gmlp_ln_gain, gmlp_ln_bias, gmlp_ws, gmlp_bs, att_rel_bias, final_norm_gain):
    batch, seq, _ = x_prompt.shape
    streams, dec_len, _ = x_sample.shape
    depth = w_in.shape[0]
    assert seq % MIXER_TT == 0 and seq >= ATT_BAND and cache_att_k.shape[2] == ATT_BAND

    tt = MIXER_TT
    n_p, n_s = batch * seq, streams * dec_len
    tm_p, tm_s = _pick_tile(n_p, 1024), _pick_tile(n_s, 512)

    w_mix_t, w_merge_t = _pack_weights(w_in, w_branch, w_out)
    gain3 = norm_gain.reshape(depth, 1, D_MODEL)
    ln_g3 = gmlp_ln_gain.reshape(depth, 1, BRANCH_W)
    ln_b3 = gmlp_ln_bias.reshape(depth, 1, BRANCH_W)
    bst = jnp.swapaxes(gmlp_bs, 1, 2)
    table_pad = jnp.pad(att_rel_bias, ((0, 0), (0, 0), (0, TABLE_LANES - att_rel_bias.shape[-1])))
    to_t = lambda c: jnp.transpose(c, (0, 1, 3, 4, 2)).reshape(depth, streams, BRANCH_W, ATT_BAND)
    cache_k, cache_v = to_t(cache_att_k), to_t(cache_att_v)
    gain2 = final_norm_gain.reshape(1, D_MODEL)

    tabs_p = _mixer_tables(np.arange(seq), tt)
    tabs_s = _mixer_tables(PAST_LEN + np.arange(dec_len), dec_len)

    hp = x_prompt.reshape(n_p, D_MODEL)
    hs = x_sample.reshape(n_s, D_MODEL)
    outs = [[] for _ in range(7)]
    h_prompt = []
    w_p = ATT_BAND + ATT_ROWS
    bias_p = _rel_bias(table_pad, ATT_BAND // ATT_ROWS + 1, ATT_ROWS, w_p, w_p, True)
    w_s = -(-(ATT_BAND + dec_len) // LANES) * LANES
    bias_s = _rel_bias(table_pad, 1, dec_len, w_s, ATT_BAND + dec_len, False)
    for l in range(depth):
        h_mix, xn = _inproj(hp, gain3, w_mix_t, l, tm_p, INPROJ_TN)
        h_prompt.append(h_mix)
        br, p_ret, p_conv = _mixer_prompt(h_mix, tabs_p, conv_w, ln_g3, ln_b3, gmlp_ws, bst, bias_p,
                                          l, batch, seq, tt)
        hp = _merge(xn, br, hp, w_merge_t, gain2, l, tm_p, l == depth - 1)

        h_mix, xn = _inproj(hs, gain3, w_mix_t, l, tm_s, INPROJ_TN)
        br, s_ret, s_conv, s_k, s_v, s_gv = _mixer_sample(h_mix, tabs_s, conv_w, ln_g3, ln_b3, gmlp_ws, bst, bias_s,
                                                          state_ret, state_conv, cache_k, cache_v,
                                                          l, streams, dec_len, _pick_tile(streams, SAMPLE_GROUP))
        hs = _merge(xn, br, hs, w_merge_t, gain2, l, tm_s, l == depth - 1)

        for dst, val in zip(outs, (p_ret, p_conv, s_ret, s_conv, s_k, s_v, s_gv)):
            dst.append(val)

    y_prompt = hp.reshape(batch, seq, D_MODEL)
    y_sample = hs.reshape(streams, dec_len, D_MODEL)
    p_ret, p_conv, s_ret, s_conv, s_k, s_v, s_gv = [jnp.stack(o) for o in outs]
    p_k, p_v = _kv_out(h_prompt, batch, seq)
    kv_shape = lambda a: a.reshape(a.shape[:3] + (ATT_HEADS, ATT_DH))
    kv_from_t = lambda a: jnp.transpose(a.reshape(a.shape[:2] + (ATT_HEADS, ATT_DH, ATT_BAND)), (0, 1, 4, 2, 3))
    return (y_prompt, y_sample, p_ret, p_conv, kv_from_t(p_k), kv_from_t(p_v),
            s_ret, s_conv, kv_shape(s_k), kv_shape(s_v), s_gv)
```

```python
import functools

import numpy as np
import jax
import jax.numpy as jnp
from jax import lax
from jax.experimental import pallas as pl
from jax.experimental.pallas import tpu as pltpu

F32 = jnp.float32
BF16 = jnp.bfloat16

D_MODEL = 2048
BRANCH_W = 512
N_BRANCH = 4
CHUNK = 64
RET_HEADS = 4
RET_DK = 64
RET_DV = 128
ROPE_BASE = 10000.0
CONV_W = 3
GMLP_BLOCK = 128
GMLP_GROUPS = 4
ATT_HEADS = 8
ATT_DH = 64
ATT_BAND = 512
MAX_REL = 128
NORM_EPS = 1e-6
PAST_LEN = 2048
NEG_BIG = -1e30

MIX_COLS = 2 * RET_HEADS * RET_DK + 13 * BRANCH_W
IN_COLS = MIX_COLS + N_BRANCH * D_MODEL
C_RQ, C_RK, C_RV, C_RG = 0, 256, 512, 1024
C_CB, C_CC, C_CX, C_CG = 1536, 2048, 2560, 3072
C_MU, C_MV, C_MG = 3584, 4096, 4608
C_AQ, C_AK, C_AV, C_AG = 5120, 5632, 6144, 6656

LANES = 128
WBLK = 256
INPROJ_TN = 7 * WBLK
ATT_ROWS = 256
MIXER_TT = 256
MERGE_ROW_CHUNK = 256
NORM_ROW_CHUNK = 256
SAMPLE_GROUP = 4
LOG2E = 1.4426950408889634
VMEM_LIMIT_BYTES = 56 * 1024 * 1024
TABLE_LANES = 384


def _cparams(n_axes):
    return pltpu.CompilerParams(dimension_semantics=("arbitrary",) * n_axes,
                                vmem_limit_bytes=VMEM_LIMIT_BYTES)


def _dot(a, b):
    return jnp.dot(a, b, preferred_element_type=F32)


def _dot_nt(a, b):
    return lax.dot_general(a, b, (((1,), (1,)), ((), ())), preferred_element_type=F32)


def _dot_tn(a, b):
    return lax.dot_general(a, b, (((0,), (0,)), ((), ())), preferred_element_type=F32)


def _silu(x):
    return x * jax.nn.sigmoid(x)


def _inproj_body(x_ref, g_ref, w_ref, h_ref, xn_ref, *, nsub):
    j = pl.program_id(1)
    tm = x_ref.shape[0]
    rc = min(tm, NORM_ROW_CHUNK)

    @pl.when(j == 0)
    def _():
        for r0 in range(0, tm, rc):
            x = x_ref[r0:r0 + rc, :]
            inv = lax.rsqrt(jnp.mean(x * x, axis=-1, keepdims=True) + NORM_EPS)
            xn = (x * inv * g_ref[0]).astype(BF16)
            xn_ref[r0:r0 + rc, :] = xn
            for c in range(nsub):
                h_ref[r0:r0 + rc, c * WBLK:(c + 1) * WBLK] = _dot(xn, w_ref[0, c]).astype(h_ref.dtype)

    @pl.when(j > 0)
    def _():
        xn = xn_ref[...]
        for c in range(nsub):
            h_ref[:, c * WBLK:(c + 1) * WBLK] = _dot(xn, w_ref[0, c]).astype(h_ref.dtype)


def _inproj(x2d, gain3, w_mix_t, layer, tm, tn):
    n = x2d.shape[0]
    nsub = tn // WBLK
    return pl.pallas_call(
        functools.partial(_inproj_body, nsub=nsub),
        grid=(n // tm, MIX_COLS // tn),
        in_specs=[
            pl.BlockSpec((tm, D_MODEL), lambda i, j: (i, 0)),
            pl.BlockSpec((1, 1, D_MODEL), lambda i, j: (layer, 0, 0)),
            pl.BlockSpec((1, nsub, D_MODEL, WBLK), lambda i, j: (layer, j, 0, 0)),
        ],
        out_specs=[
            pl.BlockSpec((tm, tn), lambda i, j: (i, j)),
            pl.BlockSpec((tm, D_MODEL), lambda i, j: (i, 0)),
        ],
        out_shape=[
            jax.ShapeDtypeStruct((n, MIX_COLS), BF16),
            jax.ShapeDtypeStruct((n, D_MODEL), BF16),
        ],
        compiler_params=_cparams(2),
        name="inproj",
    )(x2d, gain3, w_mix_t)


MERGE_ROWS = N_BRANCH * D_MODEL + N_BRANCH * BRANCH_W + D_MODEL
MERGE_WO_ROW0 = N_BRANCH * D_MODEL + N_BRANCH * BRANCH_W


def _add_out_projection(hm_sc, wo_ref, out_ref, first):
    hm = hm_sc[...]
    for j in range(D_MODEL // WBLK):
        upd = _dot(hm, wo_ref[j * WBLK:(j + 1) * WBLK, :])
        if first and j > 0:
            out_ref[:, j * WBLK:(j + 1) * WBLK] = upd
        else:
            out_ref[:, j * WBLK:(j + 1) * WBLK] += upd


def _merge_body(xn_ref, br_ref, x_ref, w_ref, *rest, final_norm):
    fg_ref, out_ref, hm_sc, wlast_sc = rest if final_norm else (None,) + rest
    s = pl.program_id(1)
    nb = D_MODEL // WBLK
    tm = out_ref.shape[0]
    wo_ref = w_ref.at[0, 0, MERGE_WO_ROW0:MERGE_ROWS]

    @pl.when(s == 0)
    def _():
        out_ref[:, 0:WBLK] = x_ref[...]
        wlast_sc[...] = wo_ref[...]

    @pl.when(s == 1)
    def _():
        _add_out_projection(hm_sc, wo_ref, out_ref, True)

    @pl.when(s > 1)
    def _():
        _add_out_projection(hm_sc, wo_ref, out_ref, False)

    rc = min(tm, MERGE_ROW_CHUNK)
    for r in range(tm // rc):
        rows = slice(r * rc, (r + 1) * rc)
        xn = xn_ref[rows, :]
        acc = None
        for i in range(N_BRANCH):
            logits = _dot(xn, w_ref[0, 0, i * D_MODEL:(i + 1) * D_MODEL, :])
            wb0 = N_BRANCH * D_MODEL + i * BRANCH_W
            proj = _dot(br_ref[rows, i * BRANCH_W:(i + 1) * BRANCH_W], w_ref[0, 0, wb0:wb0 + BRANCH_W, :])
            term = jax.nn.sigmoid(logits) * proj
            acc = term if acc is None else acc + term
        hm_sc[rows, :] = acc.astype(BF16)

    @pl.when(s == nb - 1)
    def _():
        _add_out_projection(hm_sc, wlast_sc, out_ref, False)

    for c in range(1, nb):
        @pl.when(s == c)
        def _():
            out_ref[:, c * WBLK:(c + 1) * WBLK] += x_ref[...]

    if final_norm:
        @pl.when(s == nb - 1)
        def _():
            y = out_ref[...]
            r = lax.rsqrt(jnp.mean(y * y, axis=-1, keepdims=True) + NORM_EPS)
            out_ref[...] = y * r * fg_ref[...]


def _merge(xn, br, x2d, w_merge_t, final_gain2, layer, tm, final_norm):
    n = x2d.shape[0]
    nb = D_MODEL // WBLK
    in_specs = [
        pl.BlockSpec((tm, D_MODEL), lambda m, s: (m, 0)),
        pl.BlockSpec((tm, D_MODEL), lambda m, s: (m, 0)),
        pl.BlockSpec((tm, WBLK), lambda m, s: (m, s)),
        pl.BlockSpec((1, 1, MERGE_ROWS, WBLK), lambda m, s: (layer, s, 0, 0)),
    ]
    operands = [xn, br, x2d, w_merge_t]
    if final_norm:
        in_specs.append(pl.BlockSpec((1, D_MODEL), lambda m, s: (0, 0)))
        operands.append(final_gain2)
    return pl.pallas_call(
        functools.partial(_merge_body, final_norm=final_norm),
        grid=(n // tm, nb),
        in_specs=in_specs,
        out_specs=pl.BlockSpec((tm, D_MODEL), lambda m, s: (m, 0)),
        out_shape=jax.ShapeDtypeStruct((n, D_MODEL), F32),
        scratch_shapes=[pltpu.VMEM((tm, WBLK), BF16), pltpu.VMEM((D_MODEL, WBLK), BF16)],
        compiler_params=_cparams(2),
        name="merge",
    )(*operands)


def _pack_mix_body(w_ref, o_ref, *, nsub):
    for c in range(nsub):
        o_ref[0, c] = w_ref[0, :, c * WBLK:(c + 1) * WBLK].astype(BF16)


def _pack_merge_body(g0_ref, g1_ref, g2_ref, g3_ref, wb_ref, wo_ref, o_ref):
    for i, g_ref in enumerate((g0_ref, g1_ref, g2_ref, g3_ref)):
        o_ref[0, 0, i * D_MODEL:(i + 1) * D_MODEL, :] = g_ref[0].astype(BF16)
    r0 = N_BRANCH * D_MODEL
    o_ref[0, 0, r0:r0 + N_BRANCH * BRANCH_W, :] = wb_ref[0].astype(BF16)
    r0 += N_BRANCH * BRANCH_W
    for j in range(D_MODEL // WBLK):
        o_ref[0, 0, r0 + j * WBLK:r0 + (j + 1) * WBLK, :] = wo_ref[0, :, j * WBLK:(j + 1) * WBLK].astype(BF16)


def _pack_weights(w_in, w_branch, w_out):
    depth = w_in.shape[0]
    nb = D_MODEL // WBLK
    nsub = INPROJ_TN // WBLK
    w_mix_t = pl.pallas_call(
        functools.partial(_pack_mix_body, nsub=nsub),
        grid=(depth, MIX_COLS // INPROJ_TN),
        in_specs=[pl.BlockSpec((1, D_MODEL, INPROJ_TN), lambda l, j: (l, 0, j))],
        out_specs=pl.BlockSpec((1, nsub, D_MODEL, WBLK), lambda l, j: (l, j, 0, 0)),
        out_shape=jax.ShapeDtypeStruct((depth, MIX_COLS // WBLK, D_MODEL, WBLK), BF16),
        compiler_params=_cparams(2),
        name="pack_mix",
    )(w_in)

    gate_blk0 = MIX_COLS // WBLK
    gate_spec = lambda i: pl.BlockSpec((1, D_MODEL, WBLK), lambda l, c: (l, 0, gate_blk0 + i * nb + c))
    w_merge_t = pl.pallas_call(
        _pack_merge_body,
        grid=(depth, nb),
        in_specs=[
            gate_spec(0), gate_spec(1), gate_spec(2), gate_spec(3),
            pl.BlockSpec((1, N_BRANCH * BRANCH_W, WBLK), lambda l, c: (l, 0, c)),
            pl.BlockSpec((1, WBLK, D_MODEL), lambda l, c: (l, (c + nb - 1) % nb, 0)),
        ],
        out_specs=pl.BlockSpec((1, 1, MERGE_ROWS, WBLK), lambda l, c: (l, c, 0, 0)),
        out_shape=jax.ShapeDtypeStruct((depth, nb, MERGE_ROWS, WBLK), BF16),
        compiler_params=_cparams(2),
        name="pack_merge",
    )(w_in, w_in, w_in, w_in, w_branch.reshape(depth, N_BRANCH * BRANCH_W, D_MODEL), w_out)
    return w_mix_t, w_merge_t


def _bias_body(tab_ref, perm_ref, out_ref, *, n_var, tq, w, w_valid, wpad, banded):
    t = tab_ref[0] * LOG2E
    perm = perm_ref[...]
    t_hi = t.astype(BF16)
    r1 = t - t_hi.astype(F32)
    t_mid = r1.astype(BF16)
    t_lo = (r1 - t_mid.astype(F32)).astype(BF16)
    row0 = _dot(t_hi, perm) + _dot(t_mid, perm) + _dot(t_lo, perm)
    qi = lax.broadcasted_iota(jnp.int32, (tq, w), 0)
    kj = lax.broadcasted_iota(jnp.int32, (tq, w), 1)
    visible = kj < w_valid
    if banded:
        lo = (qi // CHUNK) * CHUNK
        visible = visible & (kj >= lo) & (kj < lo + ATT_BAND + CHUNK)
    for h in range(ATT_HEADS):
        full = jnp.broadcast_to(row0[h:h + 1, :], (tq, wpad))
        shifted = pltpu.roll(full, 0, 1, stride=1, stride_axis=0)[:, :w]
        for u in range(n_var):
            vis_u = visible & (kj >= ATT_BAND - u * tq) if banded else visible
            out_ref[0, u, h] = jnp.where(vis_u, shifted, NEG_BIG)


def _bias_perm(w, wpad):
    j = np.arange(wpad)
    m = np.clip(ATT_BAND - j, -MAX_REL, MAX_REL) + MAX_REL
    m = np.where(j >= w, 2 * MAX_REL, m)
    perm = np.zeros((TABLE_LANES, wpad), np.float32)
    perm[m, j] = 1.0
    return jnp.asarray(perm, BF16)


def _rel_bias(table_pad, n_var, tq, w, w_valid, banded):
    depth = table_pad.shape[0]
    wpad = -(-(w + tq) // LANES) * LANES
    return pl.pallas_call(
        functools.partial(_bias_body, n_var=n_var, tq=tq, w=w, w_valid=w_valid, wpad=wpad, banded=banded),
        grid=(depth,),
        in_specs=[
            pl.BlockSpec((1, ATT_HEADS, TABLE_LANES), lambda l: (l, 0, 0)),
            pl.BlockSpec((TABLE_LANES, wpad), lambda l: (0, 0)),
        ],
        out_specs=pl.BlockSpec((1, n_var, ATT_HEADS, tq, w), lambda l: (l, 0, 0, 0, 0)),
        out_shape=jax.ShapeDtypeStruct((depth, n_var, ATT_HEADS, tq, w), F32),
        compiler_params=_cparams(1),
        name="rel_bias",
    )(table_pad, _bias_perm(w, wpad))


def _rotary(x, cos, sin_signed):
    lane = lax.broadcasted_iota(jnp.int32, x.shape, 1)
    first_half = (lane % RET_DK) < (RET_DK // 2)
    width = x.shape[1]
    swapped = jnp.where(first_half, pltpu.roll(x, width - RET_DK // 2, 1), pltpu.roll(x, RET_DK // 2, 1))
    return x * cos + swapped * sin_signed


def _retention(h_ref, cos, sin_signed, dmat_ref, qdec, kdec, blk_ref, smask, states, br_ref):
    length = h_ref.shape[0]
    first = lax.broadcasted_iota(jnp.int32, (length, LANES), 1) < RET_DK
    new_states = []
    for p in range(RET_HEADS // 2):
        pc = slice(p * LANES, (p + 1) * LANES)
        cols = slice(0, LANES)
        q = _rotary(h_ref[:, C_RQ + p * LANES:C_RQ + (p + 1) * LANES].astype(F32), cos[:, pc], sin_signed[:, pc])
        k = _rotary(h_ref[:, C_RK + p * LANES:C_RK + (p + 1) * LANES].astype(F32), cos[:, pc],
                    sin_signed[:, pc]) * (RET_DK ** -0.5)
        kb = k.astype(BF16)
        qd, kd = (q * qdec[:, pc]).astype(BF16), (k * kdec[:, pc]).astype(BF16)
        sp = states[p]
        v_pair = h_ref[:, C_RV + p * 2 * RET_DV:C_RV + (p + 1) * 2 * RET_DV]
        o_inter = _dot(qd[:, cols], sp.astype(BF16))
        q2 = jnp.concatenate([jnp.where(first, q[:, cols], 0.0), jnp.where(first, 0.0, q[:, cols])],
                             axis=0).astype(BF16)
        decay2 = dmat_ref[2 * p:2 * p + 2].reshape(2 * length, length)
        inner = (_dot_nt(q2, kb[:, cols]) * decay2).astype(BF16)
        o2 = _dot(inner, v_pair)
        for half in range(2):
            hh = 2 * p + half
            vl = slice(half * RET_DV, (half + 1) * RET_DV)
            o = o2[half * length:(half + 1) * length, vl] + o_inter[:, vl]
            mu = jnp.mean(o, axis=-1, keepdims=True)
            oc = o - mu
            var = jnp.mean(oc * oc, axis=-1, keepdims=True)
            hn = oc * lax.rsqrt(var + NORM_EPS)
            gcols = slice(C_RG + hh * RET_DV, C_RG + (hh + 1) * RET_DV)
            br_ref[:, hh * RET_DV:(hh + 1) * RET_DV] = (_silu(h_ref[:, gcols].astype(F32)) * hn).astype(BF16)
        upd = _dot_tn(kd[:, cols], v_pair)
        new_states.append((sp * blk_ref[p] + upd) * smask)
    return new_states


def _conv(h_ref, prev8, convw, br_ref):
    z = h_ref[:, C_CC:C_CC + BRANCH_W].astype(F32) * h_ref[:, C_CX:C_CX + BRANCH_W].astype(F32)
    zcat = jnp.concatenate([prev8, z], axis=0)
    z1 = pltpu.roll(zcat, 1, 0)[8:]
    z2 = pltpu.roll(zcat, 2, 0)[8:]
    y = convw[0:1] * z2 + convw[1:2] * z1 + convw[2:3] * z
    cb = h_ref[:, C_CB:C_CB + BRANCH_W].astype(F32)
    cg = h_ref[:, C_CG:C_CG + BRANCH_W].astype(F32)
    br_ref[:, BRANCH_W:2 * BRANCH_W] = (_silu(cg) * (cb * y)).astype(BF16)
    return z


def _gmlp(h_ref, lng, lnb, ws_ref, bst_ref, br_ref, blk_len):
    length = h_ref.shape[0]
    mv = h_ref[:, C_MV:C_MV + BRANCH_W].astype(F32)
    mean = jnp.mean(mv, axis=-1, keepdims=True)
    cen = mv - mean
    var = jnp.mean(cen * cen, axis=-1, keepdims=True)
    vn = cen * lax.rsqrt(var + NORM_EPS) * lng + lnb
    vnb = vn.astype(BF16)
    ri = lax.broadcasted_iota(jnp.int32, (blk_len, blk_len), 0)
    ci = lax.broadcasted_iota(jnp.int32, (blk_len, blk_len), 1)
    causal = (ci // CHUNK) <= (ri // CHUNK)
    n_blocks = length // blk_len
    for g in range(GMLP_GROUPS):
        wg = jnp.where(causal, ws_ref[0, g, 0:blk_len, 0:blk_len], 0.0).astype(BF16)
        bcol = bst_ref[0, 0:blk_len, g:g + 1]
        gc = slice(g * LANES, (g + 1) * LANES)
        side = jnp.concatenate([vnb[n * blk_len:(n + 1) * blk_len, gc] for n in range(n_blocks)], axis=1)
        mix_all = _dot(wg, side) + bcol
        for n in range(n_blocks):
            rows = slice(n * blk_len, (n + 1) * blk_len)
            mix = mix_all[:, n * LANES:(n + 1) * LANES]
            mu = h_ref[rows, C_MU + g * LANES:C_MU + (g + 1) * LANES].astype(F32)
            mg = h_ref[rows, C_MG + g * LANES:C_MG + (g + 1) * LANES].astype(F32)
            br_ref[rows, 2 * BRANCH_W + g * LANES:2 * BRANCH_W + (g + 1) * LANES] = (
                _silu(mg) * (mu * mix)).astype(BF16)
    return vn


def _attention(h_ref, kh_ref, vh_ref, bias_ref, br_ref, r0, rows, c0, width):
    lane = lax.broadcasted_iota(jnp.int32, (rows, LANES), 1)
    rsl = slice(r0, r0 + rows)
    for p in range(ATT_HEADS // 2):
        cols = slice(p * LANES, (p + 1) * LANES)
        qp = h_ref[rsl, C_AQ + p * LANES:C_AQ + (p + 1) * LANES].astype(F32) * (LOG2E * ATT_DH ** -0.5)
        kp = kh_ref[c0:c0 + width, cols]
        vp = vh_ref[c0:c0 + width, cols]
        first = lane < ATT_DH
        q2 = jnp.concatenate([jnp.where(first, qp, 0.0), jnp.where(first, 0.0, qp)], axis=0).astype(BF16)
        s = _dot_nt(q2, kp) + bias_ref[0, 2 * p:2 * p + 2].reshape(2 * rows, width)
        m = jnp.max(s, axis=-1, keepdims=True)
        e = jnp.exp2(s - m)
        denom = jnp.sum(e, axis=-1, keepdims=True)
        o = _dot(e.astype(BF16), vp) / denom
        acc = jnp.where(first, o[0:rows], o[rows:2 * rows])
        ag = h_ref[rsl, C_AG + p * LANES:C_AG + (p + 1) * LANES].astype(F32)
        br_ref[rsl, 3 * BRANCH_W + p * LANES:3 * BRANCH_W + (p + 1) * LANES] = (_silu(ag) * acc).astype(BF16)


def _attention_sample(h_ref, kt, vt, ak, av, bias_ref, br_ref):
    length = h_ref.shape[0]
    n_old = kt.shape[1]
    lane_head = lax.broadcasted_iota(jnp.int32, (length, BRANCH_W), 1) // ATT_DH
    q = h_ref[:, C_AQ:C_AQ + BRANCH_W].astype(F32) * (LOG2E * ATT_DH ** -0.5)
    q_st = jnp.concatenate([jnp.where(lane_head == hh, q, 0.0) for hh in range(ATT_HEADS)], axis=0).astype(BF16)
    bias = bias_ref[0, 0].reshape(ATT_HEADS * length, bias_ref.shape[-1])
    s_old = _dot(q_st, kt) + bias[:, 0:n_old]
    s_new = _dot_nt(q_st, ak) + bias[:, n_old:n_old + length]
    m = jnp.maximum(jnp.max(s_old, axis=-1, keepdims=True), jnp.max(s_new, axis=-1, keepdims=True))
    e_old = jnp.exp2(s_old - m)
    e_new = jnp.exp2(s_new - m)
    denom = jnp.sum(e_old, axis=-1, keepdims=True) + jnp.sum(e_new, axis=-1, keepdims=True)
    o_st = (_dot_nt(e_old.astype(BF16), vt) + _dot(e_new.astype(BF16), av)) / denom
    o = None
    for hh in range(ATT_HEADS):
        part = jnp.where(lane_head == hh, o_st[hh * length:(hh + 1) * length], 0.0)
        o = part if o is None else o + part
    ag = h_ref[:, C_AG:C_AG + BRANCH_W].astype(F32)
    br_ref[:, 3 * BRANCH_W:4 * BRANCH_W] = (_silu(ag) * o).astype(BF16)


def _state_to_pairs(state_ref):
    pairs = []
    zero = jnp.zeros((RET_DK, RET_DV), F32)
    for p in range(RET_HEADS // 2):
        top = jnp.concatenate([state_ref[2 * p], zero], axis=1)
        bot = jnp.concatenate([zero, state_ref[2 * p + 1]], axis=1)
        pairs.append(jnp.concatenate([top, bot], axis=0))
    return pairs


def _pairs_to_state(pairs, out_ref):
    for p in range(RET_HEADS // 2):
        out_ref[2 * p] = pairs[p][0:RET_DK, 0:RET_DV]
        out_ref[2 * p + 1] = pairs[p][RET_DK:2 * RET_DK, RET_DV:2 * RET_DV]


def _mixer_prompt_body(h_ref, cos_ref, sin_ref, dmat_ref, qdec_ref, kdec_ref, blk_ref, smask_ref,
                       convw_ref, lng_ref, lnb_ref, ws_ref, bst_ref, *rest, tt):
    n_blk = tt // ATT_ROWS
    bias_refs = rest[:n_blk]
    br_ref, sret_ref, sconv_ref, sp_sc, zc_sc, kh_sc, vh_sc = rest[n_blk:]
    t = pl.program_id(1)

    @pl.when(t == 0)
    def _():
        sp_sc[...] = jnp.zeros_like(sp_sc)
        zc_sc[...] = jnp.zeros_like(zc_sc)
        kh_sc[0:ATT_BAND] = jnp.zeros((ATT_BAND, BRANCH_W), BF16)
        vh_sc[0:ATT_BAND] = jnp.zeros((ATT_BAND, BRANCH_W), BF16)

    states = _retention(h_ref, cos_ref[...], sin_ref[...], dmat_ref, qdec_ref[...], kdec_ref[...], blk_ref,
                        smask_ref[...], [sp_sc[0], sp_sc[1]], br_ref)
    sp_sc[0] = states[0]
    sp_sc[1] = states[1]
    _pairs_to_state(states, sret_ref.at[0])

    z = _conv(h_ref, zc_sc[...], convw_ref[0], br_ref)
    tail = z[tt - (CONV_W - 1):tt]
    zc_sc[8 - (CONV_W - 1):8] = tail
    sconv_ref[0] = tail

    _gmlp(h_ref, lng_ref[0], lnb_ref[0], ws_ref, bst_ref, br_ref, GMLP_BLOCK)

    ak = h_ref[:, C_AK:C_AK + BRANCH_W]
    av = h_ref[:, C_AV:C_AV + BRANCH_W]
    kh_sc[ATT_BAND:ATT_BAND + tt] = ak
    vh_sc[ATT_BAND:ATT_BAND + tt] = av
    for blk, bias_ref in enumerate(bias_refs):
        _attention(h_ref, kh_sc, vh_sc, bias_ref.at[0], br_ref, blk * ATT_ROWS, ATT_ROWS, blk * ATT_ROWS,
                   ATT_BAND + ATT_ROWS)
    k_keep = kh_sc[tt:tt + ATT_BAND]
    v_keep = vh_sc[tt:tt + ATT_BAND]
    kh_sc[0:ATT_BAND] = k_keep
    vh_sc[0:ATT_BAND] = v_keep


def _mixer_prompt(h, tabs, conv_w, ln_g3, ln_b3, ws, bst, bias, layer, batch, seq, tt):
    nt = seq // tt
    full = lambda shape: pl.BlockSpec(shape, lambda b, t: (0,) * len(shape))
    per_layer = lambda shape: pl.BlockSpec((1,) + shape, lambda b, t: (layer,) + (0,) * len(shape))
    w = ATT_BAND + tt
    n_blk = tt // ATT_ROWS
    n_var = bias.shape[1]
    bias_spec = lambda blk: pl.BlockSpec((1, 1, ATT_HEADS, ATT_ROWS, ATT_BAND + ATT_ROWS),
                                         lambda b, t: (layer, jnp.minimum(n_blk * t + blk, n_var - 1), 0, 0, 0))
    return pl.pallas_call(
        functools.partial(_mixer_prompt_body, tt=tt),
        grid=(batch, nt),
        in_specs=[
            pl.BlockSpec((tt, MIX_COLS), lambda b, t: (b * nt + t, 0)),
            pl.BlockSpec((tt, 256), lambda b, t: (t, 0)),
            pl.BlockSpec((tt, 256), lambda b, t: (t, 0)),
            full((RET_HEADS, tt, tt)),
            full((tt, 256)), full((tt, 256)),
            full((2, 1, 256)),
            full((2 * RET_DK, 2 * RET_DV)),
            per_layer((CONV_W, BRANCH_W)),
            per_layer((1, BRANCH_W)), per_layer((1, BRANCH_W)),
            per_layer((GMLP_GROUPS, GMLP_BLOCK, GMLP_BLOCK)),
            per_layer((GMLP_BLOCK, GMLP_GROUPS)),
        ] + [bias_spec(blk) for blk in range(n_blk)],
        out_specs=[
            pl.BlockSpec((tt, D_MODEL), lambda b, t: (b * nt + t, 0)),
            pl.BlockSpec((1, RET_HEADS, RET_DK, RET_DV), lambda b, t: (b, 0, 0, 0)),
            pl.BlockSpec((1, CONV_W - 1, BRANCH_W), lambda b, t: (b, 0, 0)),
        ],
        out_shape=[
            jax.ShapeDtypeStruct((batch * seq, D_MODEL), BF16),
            jax.ShapeDtypeStruct((batch, RET_HEADS, RET_DK, RET_DV), F32),
            jax.ShapeDtypeStruct((batch, CONV_W - 1, BRANCH_W), F32),
        ],
        scratch_shapes=[
            pltpu.VMEM((2, 2 * RET_DK, 2 * RET_DV), F32),
            pltpu.VMEM((8, BRANCH_W), F32),
            pltpu.VMEM((w, BRANCH_W), BF16),
            pltpu.VMEM((w, BRANCH_W), BF16),
        ],
        compiler_params=_cparams(2),
        name="mixer_prompt",
    )(h, tabs["cos"], tabs["sin"], tabs["dmat"], tabs["qdec"], tabs["kdec"], tabs["blk"], tabs["smask"],
      conv_w, ln_g3, ln_b3, ws, bst, *([bias] * n_blk))


def _kv_out_body(*refs, depth):
    k_refs, v_refs, (ko_ref, vo_ref) = refs[:depth], refs[depth:2 * depth], refs[2 * depth:]
    for l in range(depth):
        ko_ref[l, 0] = k_refs[l][...].astype(F32).T
        vo_ref[l, 0] = v_refs[l][...].astype(F32).T


def _kv_out(h_list, batch, seq):
    depth = len(h_list)
    rows = MIXER_TT
    first = (seq - ATT_BAND) // rows
    col = lambda c0: pl.BlockSpec((rows, BRANCH_W), lambda b, r: (b * (seq // rows) + first + r, c0 // BRANCH_W))
    out_spec = pl.BlockSpec((depth, 1, BRANCH_W, rows), lambda b, r: (0, b, 0, r))
    out_shape = jax.ShapeDtypeStruct((depth, batch, BRANCH_W, ATT_BAND), F32)
    return pl.pallas_call(
        functools.partial(_kv_out_body, depth=depth),
        grid=(batch, ATT_BAND // rows),
        in_specs=[col(C_AK)] * depth + [col(C_AV)] * depth,
        out_specs=[out_spec, out_spec],
        out_shape=[out_shape, out_shape],
        compiler_params=_cparams(2),
        name="kv_out",
    )(*h_list, *h_list)


def _mixer_sample_body(h_ref, cos_ref, sin_ref, dmat_ref, qdec_ref, kdec_ref, blk_ref, smask_ref,
                       convw_ref, lng_ref, lnb_ref, ws_ref, bst_ref, bias_ref,
                       sret_in_ref, sconv_in_ref, ck_ref, cv_ref,
                       br_ref, sret_ref, sconv_ref, ko_ref, vo_ref, gv_ref, *, length, group):
    vn = _gmlp(h_ref, lng_ref[0], lnb_ref[0], ws_ref, bst_ref, br_ref, length)
    for g in range(group):
        gv_ref[g] = vn[g * length:(g + 1) * length]

    for g in range(group):
        hg = h_ref.at[g * length:(g + 1) * length]
        brg = br_ref.at[g * length:(g + 1) * length]

        states = _retention(hg, cos_ref[...], sin_ref[...], dmat_ref, qdec_ref[...], kdec_ref[...], blk_ref,
                            smask_ref[...], _state_to_pairs(sret_in_ref.at[0, g]), brg)
        _pairs_to_state(states, sret_ref.at[g])

        prev8 = jnp.concatenate([jnp.zeros((8 - (CONV_W - 1), BRANCH_W), F32), sconv_in_ref[0, g]], axis=0)
        z = _conv(hg, prev8, convw_ref[0], brg)
        sconv_ref[g] = z[length - (CONV_W - 1):length]

        ak = hg[:, C_AK:C_AK + BRANCH_W]
        av = hg[:, C_AV:C_AV + BRANCH_W]
        ko_ref[g] = ak.astype(F32)
        vo_ref[g] = av.astype(F32)
        _attention_sample(hg, ck_ref[0, g].astype(BF16), cv_ref[0, g].astype(BF16), ak, av, bias_ref, brg)


def _mixer_sample(h, tabs, conv_w, ln_g3, ln_b3, ws, bst, bias, state_ret, state_conv, cache_k, cache_v,
                  layer, streams, length, group):
    full = lambda shape: pl.BlockSpec(shape, lambda s: (0,) * len(shape))
    per_layer = lambda shape: pl.BlockSpec((1,) + shape, lambda s: (layer,) + (0,) * len(shape))
    per_stream = lambda shape: pl.BlockSpec((1, group) + shape, lambda s: (layer, s) + (0,) * len(shape))
    out_stream = lambda shape: pl.BlockSpec((group,) + shape, lambda s: (s,) + (0,) * len(shape))
    w = bias.shape[-1]
    return pl.pallas_call(
        functools.partial(_mixer_sample_body, length=length, group=group),
        grid=(streams // group,),
        in_specs=[
            pl.BlockSpec((group * length, MIX_COLS), lambda s: (s, 0)),
            full((length, 256)), full((length, 256)),
            full((RET_HEADS, length, length)),
            full((length, 256)), full((length, 256)),
            full((2, 1, 256)),
            full((2 * RET_DK, 2 * RET_DV)),
            per_layer((CONV_W, BRANCH_W)),
            per_layer((1, BRANCH_W)), per_layer((1, BRANCH_W)),
            per_layer((GMLP_GROUPS, GMLP_BLOCK, GMLP_BLOCK)),
            per_layer((GMLP_BLOCK, GMLP_GROUPS)),
            per_layer((1, ATT_HEADS, length, w)),
            per_stream((RET_HEADS, RET_DK, RET_DV)),
            per_stream((CONV_W - 1, BRANCH_W)),
            per_stream((BRANCH_W, ATT_BAND)),
            per_stream((BRANCH_W, ATT_BAND)),
        ],
        out_specs=[
            pl.BlockSpec((group * length, D_MODEL), lambda s: (s, 0)),
            out_stream((RET_HEADS, RET_DK, RET_DV)),
            out_stream((CONV_W - 1, BRANCH_W)),
            out_stream((length, BRANCH_W)),
            out_stream((length, BRANCH_W)),
            out_stream((length, BRANCH_W)),
        ],
        out_shape=[
            jax.ShapeDtypeStruct((streams * length, D_MODEL), BF16),
            jax.ShapeDtypeStruct((streams, RET_HEADS, RET_DK, RET_DV), F32),
            jax.ShapeDtypeStruct((streams, CONV_W - 1, BRANCH_W), F32),
            jax.ShapeDtypeStruct((streams, length, BRANCH_W), F32),
            jax.ShapeDtypeStruct((streams, length, BRANCH_W), F32),
            jax.ShapeDtypeStruct((streams, length, BRANCH_W), F32),
        ],
        compiler_params=_cparams(1),
        name="mixer_sample",
    )(h, tabs["cos"], tabs["sin"], tabs["dmat"], tabs["qdec"], tabs["kdec"], tabs["blk"], tabs["smask"],
      conv_w, ln_g3, ln_b3, ws, bst, bias, state_ret, state_conv, cache_k, cache_v)


def _mixer_tables(positions, blk_len):
    half = RET_DK // 2
    freqs = ROPE_BASE ** (-jnp.arange(half, dtype=F32) / half)
    ang = jnp.asarray(positions, F32)[:, None] * freqs[None, :]
    cos, sin = jnp.cos(ang), jnp.sin(ang)
    cos_t = jnp.tile(cos, (1, 2 * RET_HEADS))
    sin_t = jnp.tile(jnp.concatenate([-sin, sin], axis=1), (1, RET_HEADS))
    lg = jnp.log1p(-jnp.exp2(-5.0 - jnp.arange(RET_HEADS, dtype=F32)))
    t = jnp.arange(blk_len, dtype=F32)
    diff = t[:, None] - t[None, :]
    dmat = jnp.where(diff >= 0, jnp.exp(jnp.maximum(diff, 0.0)[None] * lg[:, None, None]), 0.0)
    qdec = jnp.repeat(jnp.exp((t[:, None] + 1.0) * lg[None, :]), RET_DK, axis=1)
    kdec = jnp.repeat(jnp.exp((blk_len - 1.0 - t)[:, None] * lg[None, :]), RET_DK, axis=1)
    blk = jnp.repeat(jnp.exp(blk_len * lg), RET_DV).reshape(RET_HEADS // 2, 1, 2 * RET_DV)
    smask = jnp.kron(jnp.eye(2, dtype=F32), jnp.ones((RET_DK, RET_DV), F32))
    return dict(cos=cos_t, sin=sin_t, dmat=dmat, qdec=qdec, kdec=kdec, blk=blk, smask=smask)


def _pick_tile(n, target):
    t = min(n, target)
    while n % t:
        t //= 2
    return t


def kernel(x_prompt, x_sample, state_ret, state_conv, cache_att_k, cache_att_v, norm_gain, w_in, w_branch, w_out,
           conv_w, gmlp_ln_gain, gmlp_ln_bias, gmlp_ws, gmlp_bs, att_rel_bias, final_norm_gain):
    batch, seq, _ = x_prompt.shape
    streams, dec_len, _ = x_sample.shape
    depth = w_in.shape[0]
    assert seq % MIXER_TT == 0 and seq >= ATT_BAND and cache_att_k.shape[2] == ATT_BAND

    tt = MIXER_TT
    n_p, n_s = batch * seq, streams * dec_len
    tm_p, tm_s = _pick_tile(n_p, 1024), _pick_tile(n_s, 512)

    w_mix_t, w_merge_t = _pack_weights(w_in, w_branch, w_out)
    gain3 = norm_gain.reshape(depth, 1, D_MODEL)
    ln_g3 = gmlp_ln_gain.reshape(depth, 1, BRANCH_W)
    ln_b3 = gmlp_ln_bias.reshape(depth, 1, BRANCH_W)
    bst = jnp.swapaxes(gmlp_bs, 1, 2)
    table_pad = jnp.pad(att_rel_bias, ((0, 0), (0, 0), (0, TABLE_LANES - att_rel_bias.shape[-1])))
    to_t = lambda c: jnp.transpose(c, (0, 1, 3, 4, 2)).reshape(depth, streams, BRANCH_W, ATT_BAND)
    cache_k, cache_v = to_t(cache_att_k), to_t(cache_att_v)
    gain2 = final_norm_gain.reshape(1, D_MODEL)

    tabs_p = _mixer_tables(np.arange(seq), tt)
    tabs_s = _mixer_tables(PAST_LEN + np.arange(dec_len), dec_len)

    hp = x_prompt.reshape(n_p, D_MODEL)
    hs = x_sample.reshape(n_s, D_MODEL)
    outs = [[] for _ in range(7)]
    h_prompt = []
    w_p = ATT_BAND + ATT_ROWS
    bias_p = _rel_bias(table_pad, ATT_BAND // ATT_ROWS + 1, ATT_ROWS, w_p, w_p, True)
    w_s = -(-(ATT_BAND + dec_len) // LANES) * LANES
    bias_s = _rel_bias(table_pad, 1, dec_len, w_s, ATT_BAND + dec_len, False)
    for l in range(depth):
        h_mix, xn = _inproj(hp, gain3, w_mix_t, l, tm_p, INPROJ_TN)
        h_prompt.append(h_mix)
        br, p_ret, p_conv = _mixer_prompt(h_mix, tabs_p, conv_w, ln_g3, ln_b3, gmlp_ws, bst, bias_p,
                                          l, batch, seq, tt)
        hp = _merge(xn, br, hp, w_merge_t, gain2, l, tm_p, l == depth - 1)

        h_mix, xn = _inproj(hs, gain3, w_mix_t, l, tm_s, INPROJ_TN)
        br, s_ret, s_conv, s_k, s_v, s_gv = _mixer_sample(h_mix, tabs_s, conv_w, ln_g3, ln_b3, gmlp_ws, bst, bias_s,
                                                          state_ret, state_conv, cache_k, cache_v,
                                                          l, streams, dec_len, _pick_tile(streams, SAMPLE_GROUP))
        hs = _merge(xn, br, hs, w_merge_t, gain2, l, tm_s, l == depth - 1)

        for dst, val in zip(outs, (p_ret, p_conv, s_ret, s_conv, s_k, s_v, s_gv)):
            dst.append(val)

    y_prompt = hp.reshape(batch, seq, D_MODEL)
    y_sample = hs.reshape(streams, dec_len, D_MODEL)
    p_ret, p_conv, s_ret, s_conv, s_k, s_v, s_gv = [jnp.stack(o) for o in outs]
    p_k, p_v = _kv_out(h_prompt, batch, seq)
    kv_shape = lambda a: a.reshape(a.shape[:3] + (ATT_HEADS, ATT_DH))
    kv_from_t = lambda a: jnp.transpose(a.reshape(a.shape[:2] + (ATT_HEADS, ATT_DH, ATT_BAND)), (0, 1, 4, 2, 3))
    return (y_prompt, y_sample, p_ret, p_conv, kv_from_t(p_k), kv_from_t(p_v),
            s_ret, s_conv, kv_shape(s_k), kv_shape(s_v), s_gv)
```
